```python
import jax, jax.numpy as jnp
from jax import lax
import numpy as np

D_MODEL = 1024
BATCH = 8
SEQ = 8192
DEPTH = 1
DEC_BATCH = 8
DEC_SEQ = 16
PAST_LEN = 2048

CHUNK = 64
D_PLE = 256
D_A = 1024
K_A = 3
D_B = 1024
K_B = 31
N_GROUPS = 4
EXPERTS_PER_GROUP = 8
N_EXPERTS = N_GROUPS * EXPERTS_PER_GROUP
TOP_K = 2
D_EXPERT = 256
EPS = 1e-6
D_IN = 3 * D_A + 2 * D_B + 2 * D_MODEL

kernel_name = "gated_conv_conformer_hmoe_stream_step"


def rmsnorm(x, g):
    xf = x.astype(jnp.float32)
    y = xf * lax.rsqrt(jnp.mean(xf * xf, axis=-1, keepdims=True) + EPS)
    return (y * g.astype(jnp.float32)).astype(x.dtype)


def layernorm(x, g, b):
    xf = x.astype(jnp.float32)
    mu = jnp.mean(xf, axis=-1, keepdims=True)
    xc = xf - mu
    y = xc * lax.rsqrt(jnp.mean(xc * xc, axis=-1, keepdims=True) + EPS)
    return (y * g.astype(jnp.float32) + b.astype(jnp.float32)).astype(x.dtype)


def causal_dwconv(u, buf, w):
    k = w.shape[0]
    c = u.shape[-1]
    xx = jnp.concatenate([buf.astype(u.dtype), u], axis=1)
    y = lax.conv_general_dilated(
        xx, w[:, None, :].astype(u.dtype), window_strides=(1,), padding="VALID",
        dimension_numbers=("NWC", "WIO", "NWC"), feature_group_count=c)
    return y, xx[:, -(k - 1):]


def hier_moe(x, w_group, b_group, w_erouter, b_erouter, w_gate_up, w_down):
    xt = x.reshape(-1, D_MODEL)
    t = xt.shape[0]
    g_logits = (xt @ w_group + b_group).astype(jnp.float32)
    g_prob = jax.nn.softmax(g_logits, axis=-1)
    g_idx = jnp.argmax(g_logits, axis=-1)
    p_g = jnp.take_along_axis(g_prob, g_idx[:, None], axis=-1)
    e_logits = (xt @ w_erouter + b_erouter).astype(jnp.float32)
    e_logits = e_logits.reshape(t, N_GROUPS, EXPERTS_PER_GROUP)
    sel = jnp.take_along_axis(e_logits, g_idx[:, None, None], axis=1)[:, 0]
    top_v, top_i = lax.top_k(sel, TOP_K)
    w_in_group = jax.nn.softmax(top_v, axis=-1)
    ids = g_idx[:, None] * EXPERTS_PER_GROUP + top_i
    comb = jnp.sum(jax.nn.one_hot(ids, N_EXPERTS, dtype=jnp.float32)
                   * (p_g * w_in_group)[..., None], axis=1).astype(x.dtype)
    out = jnp.zeros_like(xt)
    for e in range(N_EXPERTS):
        gate, up = jnp.split(xt @ w_gate_up[e], 2, axis=-1)
        out = out + comb[:, e:e + 1] * ((jax.nn.silu(gate) * up) @ w_down[e])
    return out.reshape(x.shape)


def trunk_layer(h, p, buf_a, buf_b, g_mix, w_in, b_in, conv_a_w, w_a_out,
                conv_b_w, conv_b_b, ln_b_g, ln_b_b, w_b_out, b_b_out, w_o,
                g_moe, w_group, b_group, w_erouter, b_erouter, w_gate_up, w_down,
                g_ple, w_ple_gate, w_ple_proj):
    z = rmsnorm(h, g_mix) @ w_in + b_in
    splits = [D_A, 2 * D_A, 3 * D_A, 3 * D_A + D_B, 3 * D_A + 2 * D_B,
              3 * D_A + 2 * D_B + D_MODEL]
    a_b, a_c, a_x, b_val, b_gate, gate_a, gate_b = jnp.split(z, splits, axis=-1)
    conv_a, new_a = causal_dwconv(a_c * a_x, buf_a, conv_a_w)
    ya = (a_b * conv_a) @ w_a_out
    glu = b_val * jax.nn.sigmoid(b_gate)
    conv_b, new_b = causal_dwconv(glu, buf_b, conv_b_w)
    cb = jax.nn.silu(layernorm(conv_b + conv_b_b, ln_b_g, ln_b_b))
    yb = cb @ w_b_out + b_b_out
    merged = jax.nn.sigmoid(gate_a) * ya + jax.nn.sigmoid(gate_b) * yb
    h = h + merged @ w_o
    h = h + hier_moe(rmsnorm(h, g_moe), w_group, b_group, w_erouter, b_erouter,
                     w_gate_up, w_down)
    gate = jax.nn.sigmoid(rmsnorm(h, g_ple) @ w_ple_gate)
    h = h + gate * (p.astype(h.dtype) @ w_ple_proj)
    return h, new_a, new_b


def setup_inputs(seed: int = 0) -> dict:
    key = jax.random.key(seed)
    ks = iter(jax.random.split(key, 40))
    f32 = jnp.float32

    def nrm(shape, scale):
        return jax.random.normal(next(ks), shape, f32) * scale

    def gain(shape):
        return 1.0 + 0.05 * jax.random.normal(next(ks), shape, f32)

    L = DEPTH
    return {
        "x_prompt": nrm((BATCH, SEQ, D_MODEL), 1.0),
        "x_sample": nrm((DEC_BATCH, DEC_SEQ, D_MODEL), 1.0),
        "p_prompt": nrm((DEPTH, BATCH, SEQ, D_PLE), 1.0),
        "p_sample": nrm((DEPTH, DEC_BATCH, DEC_SEQ, D_PLE), 1.0),
        "state_conv_a": nrm((DEPTH, DEC_BATCH, K_A - 1, D_A), 1.0),
        "state_conv_b": nrm((DEPTH, DEC_BATCH, K_B - 1, D_B), 1.0),
        "g_mix": gain((L, D_MODEL)),
        "w_in": nrm((L, D_MODEL, D_IN), D_MODEL ** -0.5),
        "b_in": nrm((L, D_IN), 0.02),
        "conv_a_w": nrm((L, K_A, D_A), K_A ** -0.5),
        "w_a_out": nrm((L, D_A, D_MODEL), D_A ** -0.5),
        "conv_b_w": nrm((L, K_B, D_B), K_B ** -0.5),
        "conv_b_b": nrm((L, D_B), 0.02),
        "ln_b_g": gain((L, D_B)),
        "ln_b_b": nrm((L, D_B), 0.02),
        "w_b_out": nrm((L, D_B, D_MODEL), D_B ** -0.5),
        "b_b_out": nrm((L, D_MODEL), 0.02),
        "w_o": nrm((L, D_MODEL, D_MODEL), D_MODEL ** -0.5),
        "g_moe": gain((L, D_MODEL)),
        "w_group": nrm((L, D_MODEL, N_GROUPS), D_MODEL ** -0.5),
        "b_group": nrm((L, N_GROUPS), 0.01),
        "w_erouter": nrm((L, D_MODEL, N_EXPERTS), D_MODEL ** -0.5),
        "b_erouter": nrm((L, N_EXPERTS), 0.01),
        "w_gate_up": nrm((L, N_EXPERTS, D_MODEL, 2 * D_EXPERT), D_MODEL ** -0.5),
        "w_down": nrm((L, N_EXPERTS, D_EXPERT, D_MODEL), D_EXPERT ** -0.5),
        "g_ple": gain((L, D_MODEL)),
        "w_ple_gate": nrm((L, D_MODEL, D_MODEL), D_MODEL ** -0.5),
        "w_ple_proj": nrm((L, D_PLE, D_MODEL), D_PLE ** -0.5),
        "g_final": gain((D_MODEL,)),
    }


def reference(x_prompt, x_sample, p_prompt, p_sample, state_conv_a, state_conv_b,
              g_mix, w_in, b_in, conv_a_w, w_a_out, conv_b_w, conv_b_b, ln_b_g, ln_b_b,
              w_b_out, b_b_out, w_o, g_moe, w_group, b_group, w_erouter, b_erouter,
              w_gate_up, w_down, g_ple, w_ple_gate, w_ple_proj, g_final):
    hp, hs = x_prompt, x_sample
    nb = x_prompt.shape[0]
    a_p, b_p, a_s, b_s = [], [], [], []
    for i in range(DEPTH):
        lp = dict(g_mix=g_mix[i], w_in=w_in[i], b_in=b_in[i], conv_a_w=conv_a_w[i],
                  w_a_out=w_a_out[i], conv_b_w=conv_b_w[i], conv_b_b=conv_b_b[i],
                  ln_b_g=ln_b_g[i], ln_b_b=ln_b_b[i], w_b_out=w_b_out[i],
                  b_b_out=b_b_out[i], w_o=w_o[i], g_moe=g_moe[i], w_group=w_group[i],
                  b_group=b_group[i], w_erouter=w_erouter[i], b_erouter=b_erouter[i],
                  w_gate_up=w_gate_up[i], w_down=w_down[i], g_ple=g_ple[i],
                  w_ple_gate=w_ple_gate[i], w_ple_proj=w_ple_proj[i])
        zero_a = jnp.zeros((nb, K_A - 1, D_A), hp.dtype)
        zero_b = jnp.zeros((nb, K_B - 1, D_B), hp.dtype)
        hp, na_p, nb_p = trunk_layer(hp, p_prompt[i], zero_a, zero_b, **lp)
        hs, na_s, nb_s = trunk_layer(hs, p_sample[i], state_conv_a[i], state_conv_b[i], **lp)
        a_p.append(na_p)
        b_p.append(nb_p)
        a_s.append(na_s)
        b_s.append(nb_s)
    y_prompt = rmsnorm(hp, g_final)
    y_sample = rmsnorm(hs, g_final)
    return (y_prompt, y_sample, jnp.stack(a_p), jnp.stack(b_p), jnp.stack(a_s), jnp.stack(b_s))
```

```python
import functools

import jax
import jax.numpy as jnp
from jax import lax
from jax.experimental import pallas as pl
from jax.experimental.pallas import tpu as pltpu

D_MODEL = 1024
D_PLE = 256
K_A = 3
K_B = 31
N_GROUPS = 4
EXPERTS_PER_GROUP = 8
D_EXPERT = 256
EPS = 1e-6

LANES = 128
SUBLANES = 8
HALO_A = 8
HALO_B = 32
NBLK = D_MODEL // LANES
CONV_STEPS = 4
INFO_GROUP_LANE = 0
INFO_COMB_LANE = 8
VMEM_LIMIT_BYTES = 60 * 1024 * 1024

F32 = jnp.float32
BF16 = jnp.bfloat16
NEG_BIG = -1e30


def _rms(x, g):
    ms = jnp.mean(x * x, axis=-1, keepdims=True)
    return x * lax.rsqrt(ms + EPS) * g


def _sigmoid(x):
    return 1.0 / (1.0 + jnp.exp(-x))


def _dot(a, b):
    return jnp.dot(a, b, preferred_element_type=F32)


def _rows_to_time_major(dst_ref, t0, val):
    r = val.shape[0]
    for j in range(NBLK):
        dst_ref[pl.ds(t0 * NBLK + j, r, stride=NBLK), :] = val[:, j * LANES:(j + 1) * LANES]


def _time_major_to_rows(src_ref, t0, r):
    return jnp.concatenate(
        [src_ref[pl.ds(t0 * NBLK + j, r, stride=NBLK), :] for j in range(NBLK)], axis=-1)


def _mixer_kernel(tl, nt,
                  x_ref, sa_ref, sb_ref, gmix_ref, win_ref, bin_ref, caw_ref, waout_ref,
                  cbw_ref, cbb_ref, lng_ref, lnb_ref, wbout_ref, bbout_ref, wo_ref,
                  gmoe_ref, wr_ref, br_ref,
                  h1_ref, info_ref, cnt_ref, na_ref, nb_ref,
                  cx_scr, glu_scr, conv_scr):
    d = D_MODEL
    i = pl.program_id(1)

    @pl.when(i == 0)
    def _():
        cx_scr[0:HALO_A, :] = jnp.zeros((HALO_A, d), F32)
        cx_scr[HALO_A - (K_A - 1):HALO_A, :] = sa_ref[0]
        glu_scr[0:HALO_B * NBLK, :] = jnp.zeros((HALO_B * NBLK, LANES), F32)
        _rows_to_time_major(glu_scr, HALO_B - (K_B - 1), sb_ref[0])

    @pl.when(i > 0)
    def _():
        cx_scr[0:HALO_A, :] = cx_scr[tl:tl + HALO_A, :]
        glu_scr[0:HALO_B * NBLK, :] = glu_scr[tl * NBLK:(tl + HALO_B) * NBLK, :]

    x = x_ref[0]
    n = _rms(x, gmix_ref[...]).astype(BF16)

    def seg(k):
        return _dot(n, win_ref[:, k * d:(k + 1) * d]) + bin_ref[:, k * d:(k + 1) * d]

    cx = seg(1) * seg(2)
    cx_scr[HALO_A:HALO_A + tl, :] = cx
    base_a = HALO_A - (K_A - 1)
    conv_a = caw_ref[K_A - 1:K_A, :] * cx
    for k in range(K_A - 1):
        conv_a = conv_a + caw_ref[k:k + 1, :] * cx_scr[base_a + k:base_a + k + tl, :]
    ya = _dot((seg(0) * conv_a).astype(BF16), waout_ref[...])

    glu = seg(3) * _sigmoid(seg(4))
    _rows_to_time_major(glu_scr, HALO_B, glu)
    base_b = HALO_B - (K_B - 1)
    wk = [cbw_ref[k * NBLK:(k + 1) * NBLK, :] for k in range(K_B)]
    bias = cbb_ref[...]

    def conv_body(r, carry):
        t0 = r * CONV_STEPS
        accs = [bias] * CONV_STEPS
        for j in range(K_B + CONV_STEPS - 1):
            xj = glu_scr[pl.ds(pl.multiple_of((t0 + base_b + j) * NBLK, NBLK), NBLK), :]
            for u in range(CONV_STEPS):
                if 0 <= j - u < K_B:
                    accs[u] = accs[u] + wk[j - u] * xj
        for u in range(CONV_STEPS):
            conv_scr[pl.ds(pl.multiple_of((t0 + u) * NBLK, NBLK), NBLK), :] = accs[u]
        return carry

    lax.fori_loop(0, tl // CONV_STEPS, conv_body, 0)

    cbv = _time_major_to_rows(conv_scr, 0, tl)
    mu = jnp.mean(cbv, axis=-1, keepdims=True)
    xc = cbv - mu
    var = jnp.mean(xc * xc, axis=-1, keepdims=True)
    ln = xc * lax.rsqrt(var + EPS) * lng_ref[...] + lnb_ref[...]
    cbact = (ln * _sigmoid(ln)).astype(BF16)
    yb = _dot(cbact, wbout_ref[...]) + bbout_ref[...]

    merged = _sigmoid(seg(5)) * ya + _sigmoid(seg(6)) * yb
    h1 = x + _dot(merged.astype(BF16), wo_ref[...])
    h1_ref[0] = h1

    @pl.when(i == nt - 1)
    def _():
        na_ref[0] = cx_scr[tl + HALO_A - (K_A - 1):tl + HALO_A, :]
        nb_ref[0] = _time_major_to_rows(glu_scr, tl + HALO_B - (K_B - 1), K_B - 1)

    xn = _rms(h1, gmoe_ref[...]).astype(BF16)
    logits = _dot(xn, wr_ref[...]) + br_ref[...]
    lane = lax.broadcasted_iota(jnp.int32, (tl, LANES), 1).astype(F32)
    is_g = lane < N_GROUPS
    gl = jnp.where(is_g, logits, NEG_BIG)
    gmax = jnp.max(gl, axis=-1, keepdims=True)
    gidx = jnp.min(jnp.where(gl == gmax, lane, float(LANES)), axis=-1, keepdims=True)
    gsum = jnp.sum(jnp.where(is_g, jnp.exp(gl - gmax), 0.0), axis=-1, keepdims=True)
    p_g = 1.0 / gsum
    lo = float(N_GROUPS) + gidx * float(EXPERTS_PER_GROUP)
    in_grp = (lane >= lo) & (lane < lo + float(EXPERTS_PER_GROUP))
    el = jnp.where(in_grp, logits, NEG_BIG)
    m1 = jnp.max(el, axis=-1, keepdims=True)
    i1 = jnp.min(jnp.where(el == m1, lane, float(LANES)), axis=-1, keepdims=True)
    el2 = jnp.where(lane == i1, NEG_BIG, el)
    m2 = jnp.max(el2, axis=-1, keepdims=True)
    i2 = jnp.min(jnp.where(el2 == m2, lane, float(LANES)), axis=-1, keepdims=True)
    e2 = jnp.exp(m2 - m1)
    w1 = 1.0 / (1.0 + e2)
    w2 = e2 * w1
    shift = float(INFO_COMB_LANE) - lo
    onehot = jnp.where(lane == gidx + float(INFO_GROUP_LANE), 1.0, 0.0)
    comb = (jnp.where(lane == i1 + shift, p_g * w1, 0.0)
            + jnp.where(lane == i2 + shift, p_g * w2, 0.0))
    info_ref[0] = onehot + comb
    counts = jnp.sum(onehot, axis=0, keepdims=True)
    cnt_ref[0] = jnp.broadcast_to(counts, (SUBLANES, LANES)).astype(jnp.int32)


def _const_spec(shape):
    zeros = (0,) * len(shape)
    return pl.BlockSpec(shape, lambda b, i: zeros)


def _mixer(x, sa, sb, w, tl):
    nb, l, d = x.shape
    nt = l // tl
    kern = functools.partial(_mixer_kernel, tl, nt)
    weights = [w["g_mix"], w["w_in"], w["b_in"], w["conv_a_w"], w["w_a_out"], w["conv_b_w"],
               w["conv_b_b"], w["ln_b_g"], w["ln_b_b"], w["w_b_out"], w["b_b_out"], w["w_o"],
               w["g_moe"], w["w_router"], w["b_router"]]
    in_specs = [
        pl.BlockSpec((1, tl, d), lambda b, i: (b, i, 0)),
        pl.BlockSpec((1, K_A - 1, d), lambda b, i: (b, 0, 0)),
        pl.BlockSpec((1, K_B - 1, d), lambda b, i: (b, 0, 0)),
    ] + [_const_spec(a.shape) for a in weights]
    out_shape = [
        jax.ShapeDtypeStruct((nb, l, d), F32),
        jax.ShapeDtypeStruct((nb, l, LANES), F32),
        jax.ShapeDtypeStruct((nb * nt, SUBLANES, LANES), jnp.int32),
        jax.ShapeDtypeStruct((nb, K_A - 1, d), F32),
        jax.ShapeDtypeStruct((nb, K_B - 1, d), F32),
    ]
    out_specs = [
        pl.BlockSpec((1, tl, d), lambda b, i: (b, i, 0)),
        pl.BlockSpec((1, tl, LANES), lambda b, i: (b, i, 0)),
        pl.BlockSpec((1, SUBLANES, LANES), lambda b, i: (b * nt + i, 0, 0)),
        pl.BlockSpec((1, K_A - 1, d), lambda b, i: (b, 0, 0)),
        pl.BlockSpec((1, K_B - 1, d), lambda b, i: (b, 0, 0)),
    ]
    return pl.pallas_call(
        kern,
        grid=(nb, nt),
        in_specs=in_specs,
        out_specs=out_specs,
        out_shape=out_shape,
        scratch_shapes=[
            pltpu.VMEM((HALO_A + tl, d), F32),
            pltpu.VMEM(((HALO_B + tl) * NBLK, LANES), F32),
            pltpu.VMEM((tl * NBLK, LANES), F32),
        ],
        compiler_params=pltpu.CompilerParams(
            dimension_semantics=("arbitrary", "arbitrary"),
            vmem_limit_bytes=VMEM_LIMIT_BYTES),
        name=f"mixer_tl{tl}",
    )(x, sa, sb, *weights)


def _moe_kernel(tb, ch,
                cnt_ref,
                h1_ref, info_ref, gmoe_ref, ltri_ref, wgu_ref, wd_ref,
                out_ref,
                xn_scr, acc_scr, col_scr, row_scr, comb_scr, act_scr):
    d = D_MODEL
    i = pl.program_id(0)
    g = pl.program_id(1)

    @pl.when(g == 0)
    def _():
        h = h1_ref[...]
        acc_scr[...] = h
        xn_scr[...] = _rms(h, gmoe_ref[...]).astype(BF16)
        info = info_ref[...]
        lane = lax.broadcasted_iota(jnp.int32, (tb, LANES), 1).astype(F32)
        onehot = jnp.where(lane < N_GROUPS, info, 0.0)
        before = _dot(ltri_ref[...], onehot.astype(BF16))
        mypos = jnp.sum(onehot * before, axis=-1, keepdims=True)
        mygrp = jnp.sum(onehot * lane, axis=-1, keepdims=True)
        col = jnp.where(lane == 0.0, mygrp, jnp.where(lane == 1.0, mypos, 0.0))
        col_scr[...] = col
        row_scr[...] = col.T
        hi = info.astype(BF16)
        comb_scr[:, 0:LANES] = hi
        comb_scr[:, LANES:2 * LANES] = (info - hi.astype(F32)).astype(BF16)

    n_rows = cnt_ref[i * N_GROUPS + g]
    n_chunks = (n_rows + ch - 1) // ch
    gf = g.astype(F32)
    rsel = jnp.where(row_scr[0:1, :] == gf, row_scr[1:2, :], -1.0)
    csel = jnp.where(col_scr[:, 0:1] == gf, col_scr[:, 1:2], -1.0)
    sub_iota = lax.broadcasted_iota(jnp.int32, (ch, tb), 0).astype(F32)
    lane_iota = lax.broadcasted_iota(jnp.int32, (tb, ch), 1).astype(F32)

    def chunk_body(c, carry):
        base = (c * ch).astype(F32)
        sel = jnp.where(rsel == sub_iota + base, 1.0, 0.0).astype(BF16)
        selt = jnp.where(csel == lane_iota + base, 1.0, 0.0).astype(BF16)
        xs = _dot(sel, xn_scr[...]).astype(BF16)
        cw2 = _dot(sel, comb_scr[...])
        cw = cw2[:, 0:LANES] + cw2[:, LANES:2 * LANES]
        for e in range(EXPERTS_PER_GROUP):
            hgu = _dot(xs, wgu_ref[e])
            gate = hgu[:, 0:D_EXPERT]
            up = hgu[:, D_EXPERT:2 * D_EXPERT]
            a = gate * _sigmoid(gate) * up * cw[:, INFO_COMB_LANE + e:INFO_COMB_LANE + e + 1]
            act_scr[:, e * D_EXPERT:(e + 1) * D_EXPERT] = a.astype(BF16)
        y = _dot(act_scr[...], wd_ref[...].reshape(EXPERTS_PER_GROUP * D_EXPERT, d))
        acc_scr[...] += _dot(selt, y.astype(BF16))
        return carry

    lax.fori_loop(0, n_chunks, chunk_body, 0)

    @pl.when(g == N_GROUPS - 1)
    def _():
        out_ref[...] = acc_scr[...]


def _moe(h1, info, cnt, w, tb, ch):
    n, d = h1.shape
    ntb = n // tb
    kern = functools.partial(_moe_kernel, tb, ch)
    ltri = jnp.tril(jnp.ones((tb, tb), F32), -1).astype(BF16)
    epg = EXPERTS_PER_GROUP
    grid_spec = pltpu.PrefetchScalarGridSpec(
        num_scalar_prefetch=1,
        grid=(ntb, N_GROUPS),
        in_specs=[
            pl.BlockSpec((tb, d), lambda i, g, c: (i, 0)),
            pl.BlockSpec((tb, LANES), lambda i, g, c: (i, 0)),
            pl.BlockSpec((1, d), lambda i, g, c: (0, 0)),
            pl.BlockSpec((tb, tb), lambda i, g, c: (0, 0)),
            pl.BlockSpec((epg, d, 2 * D_EXPERT), lambda i, g, c: (g, 0, 0)),
            pl.BlockSpec((epg, D_EXPERT, d), lambda i, g, c: (g, 0, 0)),
        ],
        out_specs=pl.BlockSpec((tb, d), lambda i, g, c: (i, 0)),
        scratch_shapes=[
            pltpu.VMEM((tb, d), BF16),
            pltpu.VMEM((tb, d), F32),
            pltpu.VMEM((tb, LANES), F32),
            pltpu.VMEM((LANES, tb), F32),
            pltpu.VMEM((tb, 2 * LANES), BF16),
            pltpu.VMEM((ch, epg * D_EXPERT), BF16),
        ],
    )
    return pl.pallas_call(
        kern,
        grid_spec=grid_spec,
        out_shape=jax.ShapeDtypeStruct((n, d), F32),
        compiler_params=pltpu.CompilerParams(
            dimension_semantics=("arbitrary", "arbitrary"),
            vmem_limit_bytes=VMEM_LIMIT_BYTES),
        name=f"moe_tb{tb}",
    )(cnt, h1, info, w["g_moe"], ltri, w["w_gate_up"], w["w_down"])


def _ple_kernel(h_ref, p_ref, gple_ref, wg_ref, wp_ref, gfin_ref, out_ref):
    h = h_ref[...]
    gate = _sigmoid(_dot(_rms(h, gple_ref[...]).astype(BF16), wg_ref[...]))
    h = h + gate * _dot(p_ref[...].astype(BF16), wp_ref[...])
    out_ref[...] = _rms(h, gfin_ref[...])


def _ple(h2, p, w, tc):
    n, d = h2.shape
    return pl.pallas_call(
        _ple_kernel,
        grid=(n // tc,),
        in_specs=[
            pl.BlockSpec((tc, d), lambda i: (i, 0)),
            pl.BlockSpec((tc, D_PLE), lambda i: (i, 0)),
            pl.BlockSpec((1, d), lambda i: (0, 0)),
            pl.BlockSpec((d, d), lambda i: (0, 0)),
            pl.BlockSpec((D_PLE, d), lambda i: (0, 0)),
            pl.BlockSpec((1, d), lambda i: (0, 0)),
        ],
        out_specs=pl.BlockSpec((tc, d), lambda i: (i, 0)),
        out_shape=jax.ShapeDtypeStruct((n, d), F32),
        compiler_params=pltpu.CompilerParams(
            dimension_semantics=("arbitrary",),
            vmem_limit_bytes=VMEM_LIMIT_BYTES),
        name=f"ple_tc{tc}",
    )(h2, p, w["g_ple"], w["w_ple_gate"], w["w_ple_proj"], w["g_final"])


def _tiles(nb, l):
    n = nb * l
    tl = min(l, 256)
    tb = min(n, 1024)
    return tl, tb, min(tb, 128), min(n, 512)


def _layer(x, p, sa, sb, w):
    nb, l, d = x.shape
    tl, tb, ch, tc = _tiles(nb, l)
    h1, info, cnt, na, nbuf = _mixer(x, sa, sb, w, tl)
    n = nb * l
    cnt = cnt[:, 0, :N_GROUPS].reshape(n // tb, tb // tl, N_GROUPS).sum(axis=1).reshape(-1)
    h2 = _moe(h1.reshape(n, d), info.reshape(n, LANES), cnt, w, tb, ch)
    y = _ple(h2, p.reshape(n, D_PLE), w, tc)
    return y.reshape(nb, l, d), na, nbuf


def kernel(x_prompt, x_sample, p_prompt, p_sample, state_conv_a, state_conv_b, g_mix, w_in, b_in,
           conv_a_w, w_a_out, conv_b_w, conv_b_b, ln_b_g, ln_b_b, w_b_out, b_b_out, w_o, g_moe,
           w_group, b_group, w_erouter, b_erouter, w_gate_up, w_down, g_ple, w_ple_gate,
           w_ple_proj, g_final):
    depth = g_mix.shape[0]
    assert depth == 1, "the ple kernel fuses the final norm, so only a one-layer trunk is supported"
    hp, hs = x_prompt, x_sample
    nb = x_prompt.shape[0]
    a_p, b_p, a_s, b_s = [], [], [], []
    row = lambda v: v.reshape(1, -1).astype(F32)
    for li in range(depth):
        pad = LANES - N_GROUPS - N_GROUPS * EXPERTS_PER_GROUP
        w_router = jnp.concatenate(
            [w_group[li], w_erouter[li], jnp.zeros((D_MODEL, pad), F32)], axis=1).astype(BF16)
        b_router = jnp.concatenate([b_group[li], b_erouter[li], jnp.zeros((pad,), F32)]).reshape(1, LANES)
        w = dict(
            g_mix=row(g_mix[li]), w_in=w_in[li].astype(BF16), b_in=row(b_in[li]),
            conv_a_w=conv_a_w[li], w_a_out=w_a_out[li].astype(BF16),
            conv_b_w=conv_b_w[li].reshape(K_B * NBLK, LANES),
            conv_b_b=conv_b_b[li].reshape(NBLK, LANES), ln_b_g=row(ln_b_g[li]), ln_b_b=row(ln_b_b[li]),
            w_b_out=w_b_out[li].astype(BF16), b_b_out=row(b_b_out[li]), w_o=w_o[li].astype(BF16),
            g_moe=row(g_moe[li]), w_router=w_router, b_router=b_router,
            w_gate_up=w_gate_up[li].astype(BF16), w_down=w_down[li].astype(BF16),
            g_ple=row(g_ple[li]), w_ple_gate=w_ple_gate[li].astype(BF16),
            w_ple_proj=w_ple_proj[li].astype(BF16), g_final=row(g_final))
        zero_a = jnp.zeros((nb, K_A - 1, D_MODEL), F32)
        zero_b = jnp.zeros((nb, K_B - 1, D_MODEL), F32)
        hp, na_p, nb_p = _layer(hp, p_prompt[li], zero_a, zero_b, w)
        hs, na_s, nb_s = _layer(hs, p_sample[li], state_conv_a[li], state_conv_b[li], w)
        a_p.append(na_p)
        b_p.append(nb_p)
        a_s.append(na_s)
        b_s.append(nb_s)
    return (hp, hs, jnp.stack(a_p), jnp.stack(b_p), jnp.stack(a_s), jnp.stack(b_s))
```

```python
import functools

import jax
import jax.numpy as jnp
from jax import lax
from jax.experimental import pallas as pl
from jax.experimental.pallas import tpu as pltpu

D_MODEL = 1024
D_PLE = 256
K_A = 3
K_B = 31
N_GROUPS = 4
EXPERTS_PER_GROUP = 8
D_EXPERT = 256
EPS = 1e-6

LANES = 128
SUBLANES = 8
HALO_A = 8
HALO_B = 32
NBLK = D_MODEL // LANES
CONV_STEPS = 8
Z_COLS = 256
Z_SEGS = 4
Z_TILES = Z_SEGS * D_MODEL // Z_COLS
LOOP_UNROLL = 4
N_STREAMS = 2
INFO_GROUP_LANE = 0
INFO_COMB_LANE = 8
VMEM_LIMIT_BYTES = 60 * 1024 * 1024

F32 = jnp.float32
BF16 = jnp.bfloat16
NEG_BIG = -1e30
NEG_LOG2_E = -1.4426950408889634


def _rms(x, g):
    ms = jnp.mean(x * x, axis=-1, keepdims=True)
    return x * lax.rsqrt(ms + EPS) * g


def _sigmoid(x):
    return 1.0 / (1.0 + jnp.exp2(x * NEG_LOG2_E))


def _dot(a, b):
    return jnp.dot(a, b, preferred_element_type=F32)


def _rows_to_time_major(dst_ref, t0, val):
    r = val.shape[0]
    for j in range(NBLK):
        dst_ref[pl.ds(t0 * NBLK + j, r, stride=NBLK), :] = val[:, j * LANES:(j + 1) * LANES]


def _time_major_to_rows(src_ref, t0, r):
    return jnp.concatenate(
        [src_ref[pl.ds(t0 * NBLK + j, r, stride=NBLK), :] for j in range(NBLK)], axis=-1)


def _route(logits):
    r = logits.shape[0]
    lane = lax.broadcasted_iota(jnp.int32, (r, LANES), 1).astype(F32)
    is_g = lane < N_GROUPS
    gl = jnp.where(is_g, logits, NEG_BIG)
    gmax = jnp.max(gl, axis=-1, keepdims=True)
    gidx = jnp.min(jnp.where(gl == gmax, lane, float(LANES)), axis=-1, keepdims=True)
    gsum = jnp.sum(jnp.where(is_g, jnp.exp(gl - gmax), 0.0), axis=-1, keepdims=True)
    p_g = 1.0 / gsum
    lo = float(N_GROUPS) + gidx * float(EXPERTS_PER_GROUP)
    in_grp = (lane >= lo) & (lane < lo + float(EXPERTS_PER_GROUP))
    el = jnp.where(in_grp, logits, NEG_BIG)
    m1 = jnp.max(el, axis=-1, keepdims=True)
    i1 = jnp.min(jnp.where(el == m1, lane, float(LANES)), axis=-1, keepdims=True)
    el2 = jnp.where(lane == i1, NEG_BIG, el)
    m2 = jnp.max(el2, axis=-1, keepdims=True)
    i2 = jnp.min(jnp.where(el2 == m2, lane, float(LANES)), axis=-1, keepdims=True)
    e2 = jnp.exp(m2 - m1)
    w1 = 1.0 / (1.0 + e2)
    w2 = e2 * w1
    shift = float(INFO_COMB_LANE) - lo
    onehot = jnp.where(lane == gidx + float(INFO_GROUP_LANE), 1.0, 0.0)
    comb = (jnp.where(lane == i1 + shift, p_g * w1, 0.0)
            + jnp.where(lane == i2 + shift, p_g * w2, 0.0))
    return onehot, comb


def _mixer_kernel(tl, nt, ns,
                  x_ref, sa_ref, sb_ref, gmix_ref, wvg_ref, bvg_ref, wz_ref, bz_ref, wgb_ref, bgb_ref,
                  caw_ref, waout_ref, cbw_ref, cbb_ref, lng_ref, lnb_ref, wbout_ref, bbout_ref,
                  wo_ref, gmoe_ref, wr_ref, br_ref,
                  h1_ref, info_ref, cnt_ref, na_ref, nb_ref,
                  cx_scr, glu_scr, conv_scr, n_scr, z_scr):
    d = D_MODEL
    i = pl.program_id(1)
    streams = range(ns)

    @pl.when(i == 0)
    def _():
        for s in streams:
            cx_scr[s, 0:HALO_A, :] = jnp.zeros((HALO_A, d), F32)
            cx_scr[s, HALO_A - (K_A - 1):HALO_A, :] = sa_ref[s]
            glu_scr[s, 0:HALO_B * NBLK, :] = jnp.zeros((HALO_B * NBLK, LANES), F32)
            _rows_to_time_major(glu_scr.at[s], HALO_B - (K_B - 1), sb_ref[s])

    @pl.when(i > 0)
    def _():
        for s in streams:
            cx_scr[s, 0:HALO_A, :] = cx_scr[s, tl:tl + HALO_A, :]
            glu_scr[s, 0:HALO_B * NBLK, :] = glu_scr[s, tl * NBLK:(tl + HALO_B) * NBLK, :]

    x = [x_ref[s] for s in streams]
    n = [_rms(x[s], gmix_ref[...]).astype(BF16) for s in streams]
    for s in streams:
        n_scr[s * tl:(s + 1) * tl, :] = n[s]

    for s in streams:
        vg = _dot(n[s], wvg_ref[...]) + bvg_ref[...]
        glu = vg[:, 0:d] * _sigmoid(vg[:, d:2 * d])
        _rows_to_time_major(glu_scr.at[s], HALO_B, glu)
    gate_b = [_sigmoid(_dot(n[s], wgb_ref[...]) + bgb_ref[...]) for s in streams]

    base_b = HALO_B - (K_B - 1)
    bias = cbb_ref[...]
    steps = tl // Z_TILES
    group = min(CONV_STEPS, steps)

    def loop_body(c, carry):
        for s in streams:
            for r in range(steps // group):
                t0 = c * steps + r * group
                accs = [bias] * group
                for j in range(K_B + group - 1):
                    row0 = pl.multiple_of((t0 + base_b + j) * NBLK, NBLK)
                    xj = glu_scr[s, pl.ds(row0, NBLK), :]
                    for u in range(group):
                        k = j - u
                        if 0 <= k < K_B:
                            accs[u] = accs[u] + cbw_ref[k * NBLK:(k + 1) * NBLK, :] * xj
                for u in range(group):
                    conv_scr[s, pl.ds(pl.multiple_of((t0 + u) * NBLK, NBLK), NBLK), :] = accs[u]
        z_scr[c] = _dot(n_scr[...], wz_ref[c])
        return carry

    lax.fori_loop(0, Z_TILES, loop_body, 0, unroll=LOOP_UNROLL)

    def zseg(s, k):
        per = d // Z_COLS
        return jnp.concatenate(
            [z_scr[k * per + q, s * tl:(s + 1) * tl, :] for q in range(per)], axis=-1) + bz_ref[k]

    ya = []
    for s in streams:
        cx = zseg(s, 1) * zseg(s, 2)
        cx_scr[s, HALO_A:HALO_A + tl, :] = cx
        base_a = HALO_A - (K_A - 1)
        conv_a = caw_ref[K_A - 1:K_A, :] * cx
        for k in range(K_A - 1):
            conv_a = conv_a + caw_ref[k:k + 1, :] * cx_scr[s, base_a + k:base_a + k + tl, :]
        ya.append(_dot((zseg(s, 0) * conv_a).astype(BF16), waout_ref[...]))

    yb = []
    for s in streams:
        cbv = _time_major_to_rows(conv_scr.at[s], 0, tl)
        mu = jnp.mean(cbv, axis=-1, keepdims=True)
        xc = cbv - mu
        var = jnp.mean(xc * xc, axis=-1, keepdims=True)
        ln = xc * lax.rsqrt(var + EPS) * lng_ref[...] + lnb_ref[...]
        cbact = (ln * _sigmoid(ln)).astype(BF16)
        yb.append(_dot(cbact, wbout_ref[...]) + bbout_ref[...])

    h1 = []
    for s in streams:
        merged = _sigmoid(zseg(s, 3)) * ya[s] + gate_b[s] * yb[s]
        h1.append(x[s] + _dot(merged.astype(BF16), wo_ref[...]))
        h1_ref[s] = h1[s]

    @pl.when(i == nt - 1)
    def _():
        for s in streams:
            na_ref[s] = cx_scr[s, tl + HALO_A - (K_A - 1):tl + HALO_A, :]
            nb_ref[s] = _time_major_to_rows(glu_scr.at[s], tl + HALO_B - (K_B - 1), K_B - 1)

    for s in streams:
        xn = _rms(h1[s], gmoe_ref[...]).astype(BF16)
        logits = _dot(xn, wr_ref[...]) + br_ref[...]
        onehot, comb = _route(logits)
        info_ref[s] = onehot + comb
        counts = jnp.sum(onehot, axis=0, keepdims=True)
        cnt_ref[s, 0] = jnp.broadcast_to(counts, (SUBLANES, LANES)).astype(jnp.int32)


def _const_spec(shape):
    zeros = (0,) * len(shape)
    return pl.BlockSpec(shape, lambda b, i: zeros)


def _mixer(x, sa, sb, w, tl):
    nb, l, d = x.shape
    nt = l // tl
    ns = N_STREAMS
    kern = functools.partial(_mixer_kernel, tl, nt, ns)
    weights = [w["g_mix"], w["w_vg"], w["b_vg"], w["w_z"], w["b_z"], w["w_gb"], w["b_gb"],
               w["conv_a_w"], w["w_a_out"], w["conv_b_w"], w["conv_b_b"], w["ln_b_g"], w["ln_b_b"],
               w["w_b_out"], w["b_b_out"], w["w_o"], w["g_moe"], w["w_router"], w["b_router"]]
    in_specs = [
        pl.BlockSpec((ns, tl, d), lambda b, i: (b, i, 0)),
        pl.BlockSpec((ns, K_A - 1, d), lambda b, i: (b, 0, 0)),
        pl.BlockSpec((ns, K_B - 1, d), lambda b, i: (b, 0, 0)),
    ] + [_const_spec(a.shape) for a in weights]
    out_shape = [
        jax.ShapeDtypeStruct((nb, l, d), F32),
        jax.ShapeDtypeStruct((nb, l, LANES), F32),
        jax.ShapeDtypeStruct((nb, nt, SUBLANES, LANES), jnp.int32),
        jax.ShapeDtypeStruct((nb, K_A - 1, d), F32),
        jax.ShapeDtypeStruct((nb, K_B - 1, d), F32),
    ]
    out_specs = [
        pl.BlockSpec((ns, tl, d), lambda b, i: (b, i, 0)),
        pl.BlockSpec((ns, tl, LANES), lambda b, i: (b, i, 0)),
        pl.BlockSpec((ns, 1, SUBLANES, LANES), lambda b, i: (b, i, 0, 0)),
        pl.BlockSpec((ns, K_A - 1, d), lambda b, i: (b, 0, 0)),
        pl.BlockSpec((ns, K_B - 1, d), lambda b, i: (b, 0, 0)),
    ]
    return pl.pallas_call(
        kern,
        grid=(nb // ns, nt),
        in_specs=in_specs,
        out_specs=out_specs,
        out_shape=out_shape,
        scratch_shapes=[
            pltpu.VMEM((ns, HALO_A + tl, d), F32),
            pltpu.VMEM((ns, (HALO_B + tl) * NBLK, LANES), F32),
            pltpu.VMEM((ns, tl * NBLK, LANES), F32),
            pltpu.VMEM((ns * tl, d), BF16),
            pltpu.VMEM((Z_TILES, ns * tl, Z_COLS), F32),
        ],
        compiler_params=pltpu.CompilerParams(
            dimension_semantics=("arbitrary", "arbitrary"),
            vmem_limit_bytes=VMEM_LIMIT_BYTES),
        name=f"mixer_tl{tl}",
    )(x, sa, sb, *weights)


def _moe_kernel(tb, ch,
                cnt_ref,
                h1_ref, info_ref, gmoe_ref, ltri_ref, wgu_ref, wd_ref,
                out_ref,
                xn_scr, acc_scr, col_scr, row_scr, comb_scr, act_scr):
    d = D_MODEL
    i = pl.program_id(0)
    g = pl.program_id(1)

    @pl.when(g == 0)
    def _():
        h = h1_ref[...]
        acc_scr[...] = h
        xn_scr[...] = _rms(h, gmoe_ref[...]).astype(BF16)
        info = info_ref[...]
        lane = lax.broadcasted_iota(jnp.int32, (tb, LANES), 1).astype(F32)
        onehot = jnp.where(lane < N_GROUPS, info, 0.0)
        before = _dot(ltri_ref[...], onehot.astype(BF16))
        mypos = jnp.sum(onehot * before, axis=-1, keepdims=True)
        mygrp = jnp.sum(onehot * lane, axis=-1, keepdims=True)
        col = jnp.where(lane == 0.0, mygrp, jnp.where(lane == 1.0, mypos, 0.0))
        col_scr[...] = col
        row_scr[...] = col.T
        hi = info.astype(BF16)
        comb_scr[:, 0:LANES] = hi
        comb_scr[:, LANES:2 * LANES] = (info - hi.astype(F32)).astype(BF16)

    n_rows = cnt_ref[i * N_GROUPS + g]
    n_chunks = (n_rows + ch - 1) // ch
    gf = g.astype(F32)
    rsel = jnp.where(row_scr[0:1, :] == gf, row_scr[1:2, :], -1.0)
    csel = jnp.where(col_scr[:, 0:1] == gf, col_scr[:, 1:2], -1.0)
    sub_iota = lax.broadcasted_iota(jnp.int32, (ch, tb), 0).astype(F32)
    lane_iota = lax.broadcasted_iota(jnp.int32, (tb, ch), 1).astype(F32)

    def chunk_body(c, carry):
        base = (c * ch).astype(F32)
        sel = jnp.where(rsel == sub_iota + base, 1.0, 0.0).astype(BF16)
        selt = jnp.where(csel == lane_iota + base, 1.0, 0.0).astype(BF16)
        xs = _dot(sel, xn_scr[...]).astype(BF16)
        cw2 = _dot(sel, comb_scr[...])
        cw = cw2[:, 0:LANES] + cw2[:, LANES:2 * LANES]
        for e in range(EXPERTS_PER_GROUP):
            hgu = _dot(xs, wgu_ref[e])
            gate = hgu[:, 0:D_EXPERT]
            up = hgu[:, D_EXPERT:2 * D_EXPERT]
            a = gate * _sigmoid(gate) * up * cw[:, INFO_COMB_LANE + e:INFO_COMB_LANE + e + 1]
            act_scr[:, e * D_EXPERT:(e + 1) * D_EXPERT] = a.astype(BF16)
        y = _dot(act_scr[...], wd_ref[...].reshape(EXPERTS_PER_GROUP * D_EXPERT, d))
        acc_scr[...] += _dot(selt, y.astype(BF16))
        return carry

    lax.fori_loop(0, n_chunks, chunk_body, 0)

    @pl.when(g == N_GROUPS - 1)
    def _():
        out_ref[...] = acc_scr[...]


def _moe(h1, info, cnt, w, tb, ch):
    n, d = h1.shape
    ntb = n // tb
    kern = functools.partial(_moe_kernel, tb, ch)
    ltri = jnp.tril(jnp.ones((tb, tb), F32), -1).astype(BF16)
    epg = EXPERTS_PER_GROUP
    grid_spec = pltpu.PrefetchScalarGridSpec(
        num_scalar_prefetch=1,
        grid=(ntb, N_GROUPS),
        in_specs=[
            pl.BlockSpec((tb, d), lambda i, g, c: (i, 0)),
            pl.BlockSpec((tb, LANES), lambda i, g, c: (i, 0)),
            pl.BlockSpec((1, d), lambda i, g, c: (0, 0)),
            pl.BlockSpec((tb, tb), lambda i, g, c: (0, 0)),
            pl.BlockSpec((epg, d, 2 * D_EXPERT), lambda i, g, c: (g, 0, 0)),
            pl.BlockSpec((epg, D_EXPERT, d), lambda i, g, c: (g, 0, 0)),
        ],
        out_specs=pl.BlockSpec((tb, d), lambda i, g, c: (i, 0)),
        scratch_shapes=[
            pltpu.VMEM((tb, d), BF16),
            pltpu.VMEM((tb, d), F32),
            pltpu.VMEM((tb, LANES), F32),
            pltpu.VMEM((LANES, tb), F32),
            pltpu.VMEM((tb, 2 * LANES), BF16),
            pltpu.VMEM((ch, epg * D_EXPERT), BF16),
        ],
    )
    return pl.pallas_call(
        kern,
        grid_spec=grid_spec,
        out_shape=jax.ShapeDtypeStruct((n, d), F32),
        compiler_params=pltpu.CompilerParams(
            dimension_semantics=("arbitrary", "arbitrary"),
            vmem_limit_bytes=VMEM_LIMIT_BYTES),
        name=f"moe_tb{tb}",
    )(cnt, h1, info, w["g_moe"], ltri, w["w_gate_up"], w["w_down"])


def _ple_kernel(h_ref, p_ref, gple_ref, wg_ref, wp_ref, gfin_ref, out_ref):
    h = h_ref[...]
    gate = _sigmoid(_dot(_rms(h, gple_ref[...]).astype(BF16), wg_ref[...]))
    h = h + gate * _dot(p_ref[...].astype(BF16), wp_ref[...])
    out_ref[...] = _rms(h, gfin_ref[...])


def _ple(h2, p, w, tc):
    n, d = h2.shape
    return pl.pallas_call(
        _ple_kernel,
        grid=(n // tc,),
        in_specs=[
            pl.BlockSpec((tc, d), lambda i: (i, 0)),
            pl.BlockSpec((tc, D_PLE), lambda i: (i, 0)),
            pl.BlockSpec((1, d), lambda i: (0, 0)),
            pl.BlockSpec((d, d), lambda i: (0, 0)),
            pl.BlockSpec((D_PLE, d), lambda i: (0, 0)),
            pl.BlockSpec((1, d), lambda i: (0, 0)),
        ],
        out_specs=pl.BlockSpec((tc, d), lambda i: (i, 0)),
        out_shape=jax.ShapeDtypeStruct((n, d), F32),
        compiler_params=pltpu.CompilerParams(
            dimension_semantics=("arbitrary",),
            vmem_limit_bytes=VMEM_LIMIT_BYTES),
        name=f"ple_tc{tc}",
    )(h2, p, w["g_ple"], w["w_ple_gate"], w["w_ple_proj"], w["g_final"])


def _tiles(nb, l):
    n = nb * l
    tl = min(l, 256)
    tb = min(n, 1024)
    return tl, tb, min(tb, 128), min(n, 512)


def _layer(x, p, sa, sb, w):
    nb, l, d = x.shape
    tl, tb, ch, tc = _tiles(nb, l)
    h1, info, cnt, na, nbuf = _mixer(x, sa, sb, w, tl)
    n = nb * l
    cnt = cnt[:, :, 0, :N_GROUPS].reshape(n // tb, tb // tl, N_GROUPS).sum(axis=1).reshape(-1)
    h2 = _moe(h1.reshape(n, d), info.reshape(n, LANES), cnt, w, tb, ch)
    y = _ple(h2, p.reshape(n, D_PLE), w, tc)
    return y.reshape(nb, l, d), na, nbuf


def kernel(x_prompt, x_sample, p_prompt, p_sample, state_conv_a, state_conv_b, g_mix, w_in, b_in,
           conv_a_w, w_a_out, conv_b_w, conv_b_b, ln_b_g, ln_b_b, w_b_out, b_b_out, w_o, g_moe,
           w_group, b_group, w_erouter, b_erouter, w_gate_up, w_down, g_ple, w_ple_gate,
           w_ple_proj, g_final):
    depth = g_mix.shape[0]
    assert depth == 1, "the ple kernel fuses the final norm, so only a one-layer trunk is supported"
    hp, hs = x_prompt, x_sample
    nb = x_prompt.shape[0]
    a_p, b_p, a_s, b_s = [], [], [], []
    row = lambda v: v.reshape(1, -1).astype(F32)
    for li in range(depth):
        pad = LANES - N_GROUPS - N_GROUPS * EXPERTS_PER_GROUP
        w_router = jnp.concatenate(
            [w_group[li], w_erouter[li], jnp.zeros((D_MODEL, pad), F32)], axis=1).astype(BF16)
        b_router = jnp.concatenate([b_group[li], b_erouter[li], jnp.zeros((pad,), F32)]).reshape(1, LANES)
        d = D_MODEL
        win, bi = w_in[li].astype(BF16), b_in[li]
        z_cols = lambda a: jnp.concatenate([a[..., 0:3 * d], a[..., 5 * d:6 * d]], axis=-1)
        w = dict(
            g_mix=row(g_mix[li]),
            w_vg=win[:, 3 * d:5 * d], b_vg=row(bi[3 * d:5 * d]),
            w_z=z_cols(win).reshape(d, Z_TILES, Z_COLS).transpose(1, 0, 2),
            b_z=z_cols(bi).reshape(Z_SEGS, 1, d),
            w_gb=win[:, 6 * d:7 * d], b_gb=row(bi[6 * d:7 * d]),
            conv_a_w=conv_a_w[li], w_a_out=w_a_out[li].astype(BF16),
            conv_b_w=conv_b_w[li].reshape(K_B * NBLK, LANES),
            conv_b_b=conv_b_b[li].reshape(NBLK, LANES), ln_b_g=row(ln_b_g[li]), ln_b_b=row(ln_b_b[li]),
            w_b_out=w_b_out[li].astype(BF16), b_b_out=row(b_b_out[li]), w_o=w_o[li].astype(BF16),
            g_moe=row(g_moe[li]), w_router=w_router, b_router=b_router,
            w_gate_up=w_gate_up[li].astype(BF16), w_down=w_down[li].astype(BF16),
            g_ple=row(g_ple[li]), w_ple_gate=w_ple_gate[li].astype(BF16),
            w_ple_proj=w_ple_proj[li].astype(BF16), g_final=row(g_final))
        zero_a = jnp.zeros((nb, K_A - 1, D_MODEL), F32)
        zero_b = jnp.zeros((nb, K_B - 1, D_MODEL), F32)
        hp, na_p, nb_p = _layer(hp, p_prompt[li], zero_a, zero_b, w)
        hs, na_s, nb_s = _layer(hs, p_sample[li], state_conv_a[li], state_conv_b[li], w)
        a_p.append(na_p)
        b_p.append(nb_p)
        a_s.append(na_s)
        b_s.append(nb_s)
    return (hp, hs, jnp.stack(a_p), jnp.stack(b_p), jnp.stack(a_s), jnp.stack(b_s))
```

```python
import functools

import jax
import jax.numpy as jnp
from jax import lax
from jax.experimental import pallas as pl
from jax.experimental.pallas import tpu as pltpu

D_MODEL = 1024
D_PLE = 256
K_A = 3
K_B = 31
N_GROUPS = 4
EXPERTS_PER_GROUP = 8
D_EXPERT = 256
EPS = 1e-6

MXU_K = 256
LANES = 128
SUBLANES = 8
HALO_A = 8
HALO_B = 32
NBLK = D_MODEL // LANES
TAP_PAIRS = (K_B + 1) // 2
CONV_PAIRS = 2
Z_COLS = 256
Z_SEGS = 4
Z_TILES = Z_SEGS * D_MODEL // Z_COLS
LOOP_UNROLL = 4
N_STREAMS = 2
INFO_GROUP_LANE = 0
INFO_COMB_LANE = 8
VMEM_LIMIT_BYTES = 60 * 1024 * 1024

F32 = jnp.float32
BF16 = jnp.bfloat16
NEG_BIG = -1e30
NEG_LOG2_E = -1.4426950408889634


def _rms(x, g):
    ms = jnp.mean(x * x, axis=-1, keepdims=True)
    return x * lax.rsqrt(ms + EPS) * g


def _sigmoid(x):
    return 1.0 / (1.0 + jnp.exp2(x * NEG_LOG2_E))


def _dot(a, b):
    return jnp.dot(a, b, preferred_element_type=F32)


def _rows_to_time_major(dst_ref, t0, val):
    r = val.shape[0]
    for j in range(NBLK):
        dst_ref[pl.ds(t0 * NBLK + j, r, stride=NBLK), :] = val[:, j * LANES:(j + 1) * LANES]


def _ds(start, size):
    if isinstance(start, int):
        return pl.ds(start, size)
    return pl.ds(pl.multiple_of(start, size), size)


def _time_major_to_rows(src_ref, t0, r):
    return jnp.concatenate(
        [src_ref[pl.ds(t0 * NBLK + j, r, stride=NBLK), :] for j in range(NBLK)], axis=-1)


def _route(logits):
    r = logits.shape[0]
    lane = lax.broadcasted_iota(jnp.int32, (r, LANES), 1).astype(F32)
    is_g = lane < N_GROUPS
    gl = jnp.where(is_g, logits, NEG_BIG)
    gmax = jnp.max(gl, axis=-1, keepdims=True)
    gidx = jnp.min(jnp.where(gl == gmax, lane, float(LANES)), axis=-1, keepdims=True)
    gsum = jnp.sum(jnp.where(is_g, jnp.exp(gl - gmax), 0.0), axis=-1, keepdims=True)
    p_g = 1.0 / gsum
    lo = float(N_GROUPS) + gidx * float(EXPERTS_PER_GROUP)
    in_grp = (lane >= lo) & (lane < lo + float(EXPERTS_PER_GROUP))
    el = jnp.where(in_grp, logits, NEG_BIG)
    m1 = jnp.max(el, axis=-1, keepdims=True)
    i1 = jnp.min(jnp.where(el == m1, lane, float(LANES)), axis=-1, keepdims=True)
    el2 = jnp.where(lane == i1, NEG_BIG, el)
    m2 = jnp.max(el2, axis=-1, keepdims=True)
    i2 = jnp.min(jnp.where(el2 == m2, lane, float(LANES)), axis=-1, keepdims=True)
    e2 = jnp.exp(m2 - m1)
    w1 = 1.0 / (1.0 + e2)
    w2 = e2 * w1
    shift = float(INFO_COMB_LANE) - lo
    onehot = jnp.where(lane == gidx + float(INFO_GROUP_LANE), 1.0, 0.0)
    comb = (jnp.where(lane == i1 + shift, p_g * w1, 0.0)
            + jnp.where(lane == i2 + shift, p_g * w2, 0.0))
    return onehot, comb


def _mixer_kernel(tl, nt, ns,
                  x_ref, sa_ref, sb_ref, gmix_ref, wvg_ref, bvg_ref, wz_ref, bz_ref, wgb_ref, bgb_ref,
                  caw_ref, waout_ref, wa_ref, wb_ref, cbb_ref, lng_ref, lnb_ref, wbout_ref, bbout_ref,
                  wo_ref, gmoe_ref, wr_ref, br_ref,
                  h1_ref, info_ref, cnt_ref, na_ref, nb_ref,
                  cx_scr, glu_scr, g16_scr, conv_scr, n_scr, z_scr):
    d = D_MODEL
    i = pl.program_id(1)
    streams = range(ns)

    @pl.when(i == 0)
    def _():
        for s in streams:
            cx_scr[s, 0:HALO_A, :] = jnp.zeros((HALO_A, d), F32)
            cx_scr[s, HALO_A - (K_A - 1):HALO_A, :] = sa_ref[s]
            glu_scr[s, 0:HALO_B * NBLK, :] = jnp.zeros((HALO_B * NBLK, LANES), F32)
            _rows_to_time_major(glu_scr.at[s], HALO_B - (K_B - 1), sb_ref[s])

    @pl.when(i > 0)
    def _():
        for s in streams:
            cx_scr[s, 0:HALO_A, :] = cx_scr[s, tl:tl + HALO_A, :]
            glu_scr[s, 0:HALO_B * NBLK, :] = glu_scr[s, tl * NBLK:(tl + HALO_B) * NBLK, :]

    x = [x_ref[s] for s in streams]
    n = [_rms(x[s], gmix_ref[...]).astype(BF16) for s in streams]
    for s in streams:
        n_scr[s * tl:(s + 1) * tl, :] = n[s]

    for s in streams:
        vg = _dot(n[s], wvg_ref[...]) + bvg_ref[...]
        glu = vg[:, 0:d] * _sigmoid(vg[:, d:2 * d])
        _rows_to_time_major(glu_scr.at[s], HALO_B, glu)
        g16_scr[s] = glu_scr[s].astype(BF16)
    gate_b = [_sigmoid(_dot(n[s], wgb_ref[...]) + bgb_ref[...]) for s in streams]

    base_b = HALO_B - (K_B - 1)
    bias = cbb_ref[...]
    two = 2 * NBLK

    def conv_block(s, t0):
        acc_a = [None] * CONV_PAIRS
        acc_b = [None] * CONV_PAIRS
        for q in range(TAP_PAIRS + CONV_PAIRS - 1):
            xq = g16_scr[s, _ds((t0 + base_b + 2 * q) * NBLK, two), :]
            for p in range(CONV_PAIRS):
                if 0 <= q - p < TAP_PAIRS:
                    pa = xq * wa_ref[q - p]
                    pb = xq * wb_ref[q - p]
                    acc_a[p] = pa if acc_a[p] is None else acc_a[p] + pa
                    acc_b[p] = pb if acc_b[p] is None else acc_b[p] + pb
        for p in range(CONV_PAIRS):
            a32 = acc_a[p].astype(F32)
            b32 = acc_b[p].astype(F32)
            even = a32[0:NBLK] + a32[NBLK:two] + bias
            odd = b32[0:NBLK] + b32[NBLK:two] + bias
            conv_scr[s, _ds((t0 + 2 * p) * NBLK, two), :] = jnp.concatenate([even, odd], axis=0)

    block = 2 * CONV_PAIRS
    steps = tl // Z_TILES
    conv_in_loop = steps % block == 0
    if not conv_in_loop:
        for s in streams:
            for r in range(tl // block):
                conv_block(s, r * block)

    def loop_body(c, carry):
        if conv_in_loop:
            for s in streams:
                for r in range(steps // block):
                    conv_block(s, c * steps + r * block)
        z_scr[c] = _dot(n_scr[...], wz_ref[c])
        return carry

    lax.fori_loop(0, Z_TILES, loop_body, 0, unroll=LOOP_UNROLL)

    def zseg(s, k):
        per = d // Z_COLS
        return jnp.concatenate(
            [z_scr[k * per + q, s * tl:(s + 1) * tl, :] for q in range(per)], axis=-1) + bz_ref[k]

    ya = []
    for s in streams:
        cx = zseg(s, 1) * zseg(s, 2)
        cx_scr[s, HALO_A:HALO_A + tl, :] = cx
        base_a = HALO_A - (K_A - 1)
        conv_a = caw_ref[K_A - 1:K_A, :] * cx
        for k in range(K_A - 1):
            conv_a = conv_a + caw_ref[k:k + 1, :] * cx_scr[s, base_a + k:base_a + k + tl, :]
        ya.append(_dot((zseg(s, 0) * conv_a).astype(BF16), waout_ref[...]))

    yb = []
    for s in streams:
        cbv = _time_major_to_rows(conv_scr.at[s], 0, tl)
        mu = jnp.mean(cbv, axis=-1, keepdims=True)
        xc = cbv - mu
        var = jnp.mean(xc * xc, axis=-1, keepdims=True)
        ln = xc * lax.rsqrt(var + EPS) * lng_ref[...] + lnb_ref[...]
        cbact = (ln * _sigmoid(ln)).astype(BF16)
        yb.append(_dot(cbact, wbout_ref[...]) + bbout_ref[...])

    h1 = []
    for s in streams:
        merged = _sigmoid(zseg(s, 3)) * ya[s] + gate_b[s] * yb[s]
        h1.append(x[s] + _dot(merged.astype(BF16), wo_ref[...]))
        h1_ref[s] = h1[s]

    @pl.when(i == nt - 1)
    def _():
        for s in streams:
            na_ref[s] = cx_scr[s, tl + HALO_A - (K_A - 1):tl + HALO_A, :]
            nb_ref[s] = _time_major_to_rows(glu_scr.at[s], tl + HALO_B - (K_B - 1), K_B - 1)

    for s in streams:
        xn = _rms(h1[s], gmoe_ref[...]).astype(BF16)
        logits = _dot(xn, wr_ref[...]) + br_ref[...]
        onehot, comb = _route(logits)
        info_ref[s] = onehot + comb
        counts = jnp.sum(onehot, axis=0, keepdims=True)
        cnt_ref[s, 0] = jnp.broadcast_to(counts, (SUBLANES, LANES)).astype(jnp.int32)


def _const_spec(shape):
    zeros = (0,) * len(shape)
    return pl.BlockSpec(shape, lambda b, i: zeros)


def _mixer(x, sa, sb, w, tl):
    nb, l, d = x.shape
    nt = l // tl
    ns = N_STREAMS
    kern = functools.partial(_mixer_kernel, tl, nt, ns)
    weights = [w["g_mix"], w["w_vg"], w["b_vg"], w["w_z"], w["b_z"], w["w_gb"], w["b_gb"],
               w["conv_a_w"], w["w_a_out"], w["conv_b_wa"], w["conv_b_wb"], w["conv_b_b"],
               w["ln_b_g"], w["ln_b_b"],
               w["w_b_out"], w["b_b_out"], w["w_o"], w["g_moe"], w["w_router"], w["b_router"]]
    in_specs = [
        pl.BlockSpec((ns, tl, d), lambda b, i: (b, i, 0)),
        pl.BlockSpec((ns, K_A - 1, d), lambda b, i: (b, 0, 0)),
        pl.BlockSpec((ns, K_B - 1, d), lambda b, i: (b, 0, 0)),
    ] + [_const_spec(a.shape) for a in weights]
    out_shape = [
        jax.ShapeDtypeStruct((nb, l, d), F32),
        jax.ShapeDtypeStruct((nb, l, LANES), F32),
        jax.ShapeDtypeStruct((nb, nt, SUBLANES, LANES), jnp.int32),
        jax.ShapeDtypeStruct((nb, K_A - 1, d), F32),
        jax.ShapeDtypeStruct((nb, K_B - 1, d), F32),
    ]
    out_specs = [
        pl.BlockSpec((ns, tl, d), lambda b, i: (b, i, 0)),
        pl.BlockSpec((ns, tl, LANES), lambda b, i: (b, i, 0)),
        pl.BlockSpec((ns, 1, SUBLANES, LANES), lambda b, i: (b, i, 0, 0)),
        pl.BlockSpec((ns, K_A - 1, d), lambda b, i: (b, 0, 0)),
        pl.BlockSpec((ns, K_B - 1, d), lambda b, i: (b, 0, 0)),
    ]
    return pl.pallas_call(
        kern,
        grid=(nb // ns, nt),
        in_specs=in_specs,
        out_specs=out_specs,
        out_shape=out_shape,
        scratch_shapes=[
            pltpu.VMEM((ns, HALO_A + tl, d), F32),
            pltpu.VMEM((ns, (HALO_B + tl) * NBLK, LANES), F32),
            pltpu.VMEM((ns, (HALO_B + tl) * NBLK, LANES), BF16),
            pltpu.VMEM((ns, tl * NBLK, LANES), F32),
            pltpu.VMEM((ns * tl, d), BF16),
            pltpu.VMEM((Z_TILES, ns * tl, Z_COLS), F32),
        ],
        compiler_params=pltpu.CompilerParams(
            dimension_semantics=("arbitrary", "arbitrary"),
            vmem_limit_bytes=VMEM_LIMIT_BYTES),
        name=f"mixer_tl{tl}",
    )(x, sa, sb, *weights)


def _moe_kernel(tb, ch,
                cnt_ref,
                h1_ref, info_ref, gmoe_ref, ltri_ref, wgu_ref, wd_ref,
                out_ref,
                xn_scr, col_scr, row_scr, comb_scr, act_scr, y_scr):
    d = D_MODEL
    i = pl.program_id(0)
    g = pl.program_id(1)
    chunks = [(cnt_ref[i * N_GROUPS + k] + ch - 1) // ch for k in range(N_GROUPS)]
    starts = [sum(chunks[:k]) * ch for k in range(N_GROUPS)]

    @pl.when((i == 0) & (g == 0))
    def _():
        y_scr[...] = jnp.zeros(y_scr.shape, BF16)

    @pl.when(g == 0)
    def _():
        xn_scr[...] = _rms(h1_ref[...], gmoe_ref[...]).astype(BF16)
        info = info_ref[...]
        lane = lax.broadcasted_iota(jnp.int32, (tb, LANES), 1).astype(F32)
        onehot = jnp.where(lane < N_GROUPS, info, 0.0)
        before = _dot(ltri_ref[...], onehot.astype(BF16))
        mypos = jnp.sum(onehot * before, axis=-1, keepdims=True)
        mygrp = jnp.sum(onehot * lane, axis=-1, keepdims=True)
        slot = mypos
        for k in range(1, N_GROUPS):
            slot = slot + jnp.where(mygrp == float(k), starts[k].astype(F32), 0.0)
        col = jnp.where(lane == 0.0, mygrp,
                        jnp.where(lane == 1.0, mypos, jnp.where(lane == 2.0, slot, 0.0)))
        col_scr[...] = col
        row_scr[...] = col.T
        hi = info.astype(BF16)
        comb_scr[:, 0:LANES] = hi
        comb_scr[:, LANES:2 * LANES] = (info - hi.astype(F32)).astype(BF16)

    n_chunks = chunks[0]
    start = starts[0]
    for k in range(1, N_GROUPS):
        n_chunks = jnp.where(g == k, chunks[k], n_chunks)
        start = jnp.where(g == k, starts[k], start)
    gf = g.astype(F32)
    rsel = jnp.where(row_scr[0:1, :] == gf, row_scr[1:2, :], -1.0)
    sub_iota = lax.broadcasted_iota(jnp.int32, (ch, tb), 0).astype(F32)

    def chunk_body(c, carry):
        base = (c * ch).astype(F32)
        sel = jnp.where(rsel == sub_iota + base, 1.0, 0.0).astype(BF16)
        xs = _dot(sel, xn_scr[...]).astype(BF16)
        cw2 = _dot(sel, comb_scr[...])
        cw = cw2[:, 0:LANES] + cw2[:, LANES:2 * LANES]
        for e in range(EXPERTS_PER_GROUP):
            hgu = _dot(xs, wgu_ref[e])
            gate = hgu[:, 0:D_EXPERT]
            up = hgu[:, D_EXPERT:2 * D_EXPERT]
            a = gate * _sigmoid(gate) * up * cw[:, INFO_COMB_LANE + e:INFO_COMB_LANE + e + 1]
            act_scr[:, e * D_EXPERT:(e + 1) * D_EXPERT] = a.astype(BF16)
        y = _dot(act_scr[...], wd_ref[...].reshape(EXPERTS_PER_GROUP * D_EXPERT, d))
        y_scr[pl.ds(pl.multiple_of(start + c * ch, ch), ch), :] = y.astype(BF16)
        return carry

    lax.fori_loop(0, n_chunks, chunk_body, 0)

    @pl.when(g == N_GROUPS - 1)
    def _():
        yrows = y_scr.shape[0]
        lane_iota = lax.broadcasted_iota(jnp.int32, (tb, yrows), 1).astype(F32)
        selt = jnp.where(col_scr[:, 2:3] == lane_iota, 1.0, 0.0).astype(BF16)
        out_ref[...] = h1_ref[...] + _dot(selt, y_scr[...])


def _moe(h1, info, cnt, w, tb, ch):
    n, d = h1.shape
    ntb = n // tb
    kern = functools.partial(_moe_kernel, tb, ch)
    yrows = -(-(tb + N_GROUPS * ch) // MXU_K) * MXU_K
    ltri = jnp.tril(jnp.ones((tb, tb), F32), -1).astype(BF16)
    epg = EXPERTS_PER_GROUP
    grid_spec = pltpu.PrefetchScalarGridSpec(
        num_scalar_prefetch=1,
        grid=(ntb, N_GROUPS),
        in_specs=[
            pl.BlockSpec((tb, d), lambda i, g, c: (i, 0)),
            pl.BlockSpec((tb, LANES), lambda i, g, c: (i, 0)),
            pl.BlockSpec((1, d), lambda i, g, c: (0, 0)),
            pl.BlockSpec((tb, tb), lambda i, g, c: (0, 0)),
            pl.BlockSpec((epg, d, 2 * D_EXPERT), lambda i, g, c: (g, 0, 0)),
            pl.BlockSpec((epg, D_EXPERT, d), lambda i, g, c: (g, 0, 0)),
        ],
        out_specs=pl.BlockSpec((tb, d), lambda i, g, c: (i, 0)),
        scratch_shapes=[
            pltpu.VMEM((tb, d), BF16),
            pltpu.VMEM((tb, LANES), F32),
            pltpu.VMEM((LANES, tb), F32),
            pltpu.VMEM((tb, 2 * LANES), BF16),
            pltpu.VMEM((ch, epg * D_EXPERT), BF16),
            pltpu.VMEM((yrows, d), BF16),
        ],
    )
    return pl.pallas_call(
        kern,
        grid_spec=grid_spec,
        out_shape=jax.ShapeDtypeStruct((n, d), F32),
        compiler_params=pltpu.CompilerParams(
            dimension_semantics=("arbitrary", "arbitrary"),
            vmem_limit_bytes=VMEM_LIMIT_BYTES),
        name=f"moe_tb{tb}",
    )(cnt, h1, info, w["g_moe"], ltri, w["w_gate_up"], w["w_down"])


def _ple_kernel(h_ref, p_ref, gple_ref, wg_ref, wp_ref, gfin_ref, out_ref):
    h = h_ref[...]
    gate = _sigmoid(_dot(_rms(h, gple_ref[...]).astype(BF16), wg_ref[...]))
    h = h + gate * _dot(p_ref[...].astype(BF16), wp_ref[...])
    out_ref[...] = _rms(h, gfin_ref[...])


def _ple(h2, p, w, tc):
    n, d = h2.shape
    return pl.pallas_call(
        _ple_kernel,
        grid=(n // tc,),
        in_specs=[
            pl.BlockSpec((tc, d), lambda i: (i, 0)),
            pl.BlockSpec((tc, D_PLE), lambda i: (i, 0)),
            pl.BlockSpec((1, d), lambda i: (0, 0)),
            pl.BlockSpec((d, d), lambda i: (0, 0)),
            pl.BlockSpec((D_PLE, d), lambda i: (0, 0)),
            pl.BlockSpec((1, d), lambda i: (0, 0)),
        ],
        out_specs=pl.BlockSpec((tc, d), lambda i: (i, 0)),
        out_shape=jax.ShapeDtypeStruct((n, d), F32),
        compiler_params=pltpu.CompilerParams(
            dimension_semantics=("arbitrary",),
            vmem_limit_bytes=VMEM_LIMIT_BYTES),
        name=f"ple_tc{tc}",
    )(h2, p, w["g_ple"], w["w_ple_gate"], w["w_ple_proj"], w["g_final"])


def _tiles(nb, l):
    n = nb * l
    tl = min(l, 256)
    tb = min(n, 1024)
    return tl, tb, min(tb, 128), min(n, 512)


def _layer(x, p, sa, sb, w):
    nb, l, d = x.shape
    tl, tb, ch, tc = _tiles(nb, l)
    h1, info, cnt, na, nbuf = _mixer(x, sa, sb, w, tl)
    n = nb * l
    cnt = cnt[:, :, 0, :N_GROUPS].reshape(n // tb, tb // tl, N_GROUPS).sum(axis=1).reshape(-1)
    h2 = _moe(h1.reshape(n, d), info.reshape(n, LANES), cnt, w, tb, ch)
    y = _ple(h2, p.reshape(n, D_PLE), w, tc)
    return y.reshape(nb, l, d), na, nbuf


def kernel(x_prompt, x_sample, p_prompt, p_sample, state_conv_a, state_conv_b, g_mix, w_in, b_in,
           conv_a_w, w_a_out, conv_b_w, conv_b_b, ln_b_g, ln_b_b, w_b_out, b_b_out, w_o, g_moe,
           w_group, b_group, w_erouter, b_erouter, w_gate_up, w_down, g_ple, w_ple_gate,
           w_ple_proj, g_final):
    depth = g_mix.shape[0]
    assert depth == 1, "the ple kernel fuses the final norm, so only a one-layer trunk is supported"
    hp, hs = x_prompt, x_sample
    nb = x_prompt.shape[0]
    a_p, b_p, a_s, b_s = [], [], [], []
    row = lambda v: v.reshape(1, -1).astype(F32)
    for li in range(depth):
        pad = LANES - N_GROUPS - N_GROUPS * EXPERTS_PER_GROUP
        w_router = jnp.concatenate(
            [w_group[li], w_erouter[li], jnp.zeros((D_MODEL, pad), F32)], axis=1).astype(BF16)
        b_router = jnp.concatenate([b_group[li], b_erouter[li], jnp.zeros((pad,), F32)]).reshape(1, LANES)
        d = D_MODEL
        win, bi = w_in[li].astype(BF16), b_in[li]
        z_cols = lambda a: jnp.concatenate([a[..., 0:3 * d], a[..., 5 * d:6 * d]], axis=-1)
        taps = jnp.pad(conv_b_w[li].reshape(K_B, NBLK, LANES), ((1, 1), (0, 0), (0, 0)))
        tap_pairs = lambda o: jnp.stack(
            [jnp.concatenate([taps[2 * q + o + 1], taps[2 * q + o + 2]], axis=0)
             for q in range(TAP_PAIRS)]).astype(BF16)
        w = dict(
            g_mix=row(g_mix[li]),
            w_vg=win[:, 3 * d:5 * d], b_vg=row(bi[3 * d:5 * d]),
            w_z=z_cols(win).reshape(d, Z_TILES, Z_COLS).transpose(1, 0, 2),
            b_z=z_cols(bi).reshape(Z_SEGS, 1, d),
            w_gb=win[:, 6 * d:7 * d], b_gb=row(bi[6 * d:7 * d]),
            conv_a_w=conv_a_w[li], w_a_out=w_a_out[li].astype(BF16),
            conv_b_wa=tap_pairs(0), conv_b_wb=tap_pairs(-1),
            conv_b_b=conv_b_b[li].reshape(NBLK, LANES), ln_b_g=row(ln_b_g[li]), ln_b_b=row(ln_b_b[li]),
            w_b_out=w_b_out[li].astype(BF16), b_b_out=row(b_b_out[li]), w_o=w_o[li].astype(BF16),
            g_moe=row(g_moe[li]), w_router=w_router, b_router=b_router,
            w_gate_up=w_gate_up[li].astype(BF16), w_down=w_down[li].astype(BF16),
            g_ple=row(g_ple[li]), w_ple_gate=w_ple_gate[li].astype(BF16),
            w_ple_proj=w_ple_proj[li].astype(BF16), g_final=row(g_final))
        zero_a = jnp.zeros((nb, K_A - 1, D_MODEL), F32)
        zero_b = jnp.zeros((nb, K_B - 1, D_MODEL), F32)
        hp, na_p, nb_p = _layer(hp, p_prompt[li], zero_a, zero_b, w)
        hs, na_s, nb_s = _layer(hs, p_sample[li], state_conv_a[li], state_conv_b[li], w)
        a_p.append(na_p)
        b_p.append(nb_p)
        a_s.append(na_s)
        b_s.append(nb_s)
    return (hp, hs, jnp.stack(a_p), jnp.stack(b_p), jnp.stack(a_s), jnp.stack(b_s))
```

```python
import functools

import jax
import jax.numpy as jnp
from jax import lax
from jax.experimental import pallas as pl
from jax.experimental.pallas import tpu as pltpu

D_MODEL = 1024
D_PLE = 256
K_A = 3
K_B = 31
N_GROUPS = 4
EXPERTS_PER_GROUP = 8
D_EXPERT = 256
EPS = 1e-6

MXU_K = 256
LANES = 128
SUBLANES = 8
HALO_A = 8
HALO_B = 32
NBLK = D_MODEL // LANES
CONV_STEPS = 8
CONV_UNROLL = 2
Z_SEGS = 4
N_STREAMS = 2
OUT_ROWS = 256
INFO_GROUP_LANE = 0
INFO_COMB_LANE = 8
VMEM_LIMIT_BYTES = 60 * 1024 * 1024

F32 = jnp.float32
BF16 = jnp.bfloat16
NEG_BIG = -1e30
NEG_LOG2_E = -1.4426950408889634


def _rms(x, g):
    ms = jnp.mean(x * x, axis=-1, keepdims=True)
    return x * lax.rsqrt(ms + EPS) * g


def _sigmoid(x):
    return 1.0 / (1.0 + jnp.exp2(x * NEG_LOG2_E))


def _dot(a, b):
    return jnp.dot(a, b, preferred_element_type=F32)


def _rows_to_time_major(dst_ref, t0, val):
    r = val.shape[0]
    for j in range(NBLK):
        dst_ref[pl.ds(t0 * NBLK + j, r, stride=NBLK), :] = val[:, j * LANES:(j + 1) * LANES]


def _ds(start, size):
    if isinstance(start, int):
        return pl.ds(start, size)
    return pl.ds(pl.multiple_of(start, size), size)


def _time_major_to_rows(src_ref, t0, r):
    return jnp.concatenate(
        [src_ref[pl.ds(t0 * NBLK + j, r, stride=NBLK), :] for j in range(NBLK)], axis=-1)


def _route(logits):
    r = logits.shape[0]
    lane = lax.broadcasted_iota(jnp.int32, (r, LANES), 1).astype(F32)
    is_g = lane < N_GROUPS
    gl = jnp.where(is_g, logits, NEG_BIG)
    gmax = jnp.max(gl, axis=-1, keepdims=True)
    gidx = jnp.min(jnp.where(gl == gmax, lane, float(LANES)), axis=-1, keepdims=True)
    gsum = jnp.sum(jnp.where(is_g, jnp.exp(gl - gmax), 0.0), axis=-1, keepdims=True)
    p_g = 1.0 / gsum
    lo = float(N_GROUPS) + gidx * float(EXPERTS_PER_GROUP)
    in_grp = (lane >= lo) & (lane < lo + float(EXPERTS_PER_GROUP))
    el = jnp.where(in_grp, logits, NEG_BIG)
    m1 = jnp.max(el, axis=-1, keepdims=True)
    i1 = jnp.min(jnp.where(el == m1, lane, float(LANES)), axis=-1, keepdims=True)
    el2 = jnp.where(lane == i1, NEG_BIG, el)
    m2 = jnp.max(el2, axis=-1, keepdims=True)
    i2 = jnp.min(jnp.where(el2 == m2, lane, float(LANES)), axis=-1, keepdims=True)
    e2 = jnp.exp(m2 - m1)
    w1 = 1.0 / (1.0 + e2)
    w2 = e2 * w1
    shift = float(INFO_COMB_LANE) - lo
    onehot = jnp.where(lane == gidx + float(INFO_GROUP_LANE), 1.0, 0.0)
    comb = (jnp.where(lane == i1 + shift, p_g * w1, 0.0)
            + jnp.where(lane == i2 + shift, p_g * w2, 0.0))
    return onehot, comb


def _mixer_kernel(tl, nt, ns,
                  x_ref, sa_ref, sb_ref, gmix_ref, wvg_ref, bvg_ref, wz_ref, bz_ref, wgb_ref, bgb_ref,
                  caw_ref, waout_ref, cbw_ref, cbb_ref, lng_ref, lnb_ref, wbout_ref, bbout_ref,
                  wo_ref, gmoe_ref, wr_ref, br_ref,
                  h1_ref, info_ref, cnt_ref, na_ref, nb_ref,
                  cx_scr, glu_scr, conv_scr):
    d = D_MODEL
    i = pl.program_id(1)
    streams = range(ns)

    @pl.when(i == 0)
    def _():
        for s in streams:
            cx_scr[s, 0:HALO_A, :] = jnp.zeros((HALO_A, d), F32)
            cx_scr[s, HALO_A - (K_A - 1):HALO_A, :] = sa_ref[s]
            glu_scr[s, 0:HALO_B * NBLK, :] = jnp.zeros((HALO_B * NBLK, LANES), F32)
            _rows_to_time_major(glu_scr.at[s], HALO_B - (K_B - 1), sb_ref[s])

    @pl.when(i > 0)
    def _():
        for s in streams:
            cx_scr[s, 0:HALO_A, :] = cx_scr[s, tl:tl + HALO_A, :]
            glu_scr[s, 0:HALO_B * NBLK, :] = glu_scr[s, tl * NBLK:(tl + HALO_B) * NBLK, :]

    x = [x_ref[s] for s in streams]
    n = [_rms(x[s], gmix_ref[...]).astype(BF16) for s in streams]

    for s in streams:
        vg = _dot(n[s], wvg_ref[...]) + bvg_ref[...]
        glu = vg[:, 0:d] * _sigmoid(vg[:, d:2 * d])
        _rows_to_time_major(glu_scr.at[s], HALO_B, glu)

    base_b = HALO_B - (K_B - 1)
    bias = cbb_ref[...]
    group = min(CONV_STEPS, tl)

    def conv_body(c, carry):
        for s in streams:
            t0 = c * group
            accs = [bias] * group
            for j in range(K_B + group - 1):
                xj = glu_scr[s, _ds((t0 + base_b + j) * NBLK, NBLK), :]
                for u in range(group):
                    k = j - u
                    if 0 <= k < K_B:
                        accs[u] = accs[u] + cbw_ref[k * NBLK:(k + 1) * NBLK, :] * xj
            for u in range(group):
                conv_scr[s, _ds((t0 + u) * NBLK, NBLK), :] = accs[u]
        return carry

    lax.fori_loop(0, tl // group, conv_body, 0, unroll=CONV_UNROLL)

    def zseg(s, k):
        return _dot(n[s], wz_ref[k]) + bz_ref[k]

    ya = []
    for s in streams:
        cx = zseg(s, 1) * zseg(s, 2)
        cx_scr[s, HALO_A:HALO_A + tl, :] = cx
        base_a = HALO_A - (K_A - 1)
        conv_a = caw_ref[K_A - 1:K_A, :] * cx
        for k in range(K_A - 1):
            conv_a = conv_a + caw_ref[k:k + 1, :] * cx_scr[s, base_a + k:base_a + k + tl, :]
        ya.append(_dot((zseg(s, 0) * conv_a).astype(BF16), waout_ref[...]))

    yb = []
    for s in streams:
        cbv = _time_major_to_rows(conv_scr.at[s], 0, tl)
        mu = jnp.mean(cbv, axis=-1, keepdims=True)
        xc = cbv - mu
        var = jnp.mean(xc * xc, axis=-1, keepdims=True)
        ln = xc * lax.rsqrt(var + EPS) * lng_ref[...] + lnb_ref[...]
        cbact = (ln * _sigmoid(ln)).astype(BF16)
        yb.append(_dot(cbact, wbout_ref[...]) + bbout_ref[...])

    h1 = []
    for s in streams:
        gate_b = _sigmoid(_dot(n[s], wgb_ref[...]) + bgb_ref[...])
        merged = _sigmoid(zseg(s, 3)) * ya[s] + gate_b * yb[s]
        h1.append(x[s] + _dot(merged.astype(BF16), wo_ref[...]))
        h1_ref[s] = h1[s]

    @pl.when(i == nt - 1)
    def _():
        for s in streams:
            na_ref[s] = cx_scr[s, tl + HALO_A - (K_A - 1):tl + HALO_A, :]
            nb_ref[s] = _time_major_to_rows(glu_scr.at[s], tl + HALO_B - (K_B - 1), K_B - 1)

    for s in streams:
        xn = _rms(h1[s], gmoe_ref[...]).astype(BF16)
        logits = _dot(xn, wr_ref[...]) + br_ref[...]
        onehot, comb = _route(logits)
        info_ref[s] = onehot + comb
        counts = jnp.sum(onehot, axis=0, keepdims=True)
        cnt_ref[s, 0] = jnp.broadcast_to(counts, (SUBLANES, LANES)).astype(jnp.int32)


def _const_spec(shape):
    zeros = (0,) * len(shape)
    return pl.BlockSpec(shape, lambda b, i: zeros)


def _mixer(x, sa, sb, w, tl):
    nb, l, d = x.shape
    nt = l // tl
    ns = N_STREAMS
    kern = functools.partial(_mixer_kernel, tl, nt, ns)
    weights = [w["g_mix"], w["w_vg"], w["b_vg"], w["w_z"], w["b_z"], w["w_gb"], w["b_gb"],
               w["conv_a_w"], w["w_a_out"], w["conv_b_w"], w["conv_b_b"], w["ln_b_g"], w["ln_b_b"],
               w["w_b_out"], w["b_b_out"], w["w_o"], w["g_moe"], w["w_router"], w["b_router"]]
    in_specs = [
        pl.BlockSpec((ns, tl, d), lambda b, i: (b, i, 0)),
        pl.BlockSpec((ns, K_A - 1, d), lambda b, i: (b, 0, 0)),
        pl.BlockSpec((ns, K_B - 1, d), lambda b, i: (b, 0, 0)),
    ] + [_const_spec(a.shape) for a in weights]
    out_shape = [
        jax.ShapeDtypeStruct((nb, l, d), F32),
        jax.ShapeDtypeStruct((nb, l, LANES), F32),
        jax.ShapeDtypeStruct((nb, nt, SUBLANES, LANES), jnp.int32),
        jax.ShapeDtypeStruct((nb, K_A - 1, d), F32),
        jax.ShapeDtypeStruct((nb, K_B - 1, d), F32),
    ]
    out_specs = [
        pl.BlockSpec((ns, tl, d), lambda b, i: (b, i, 0)),
        pl.BlockSpec((ns, tl, LANES), lambda b, i: (b, i, 0)),
        pl.BlockSpec((ns, 1, SUBLANES, LANES), lambda b, i: (b, i, 0, 0)),
        pl.BlockSpec((ns, K_A - 1, d), lambda b, i: (b, 0, 0)),
        pl.BlockSpec((ns, K_B - 1, d), lambda b, i: (b, 0, 0)),
    ]
    return pl.pallas_call(
        kern,
        grid=(nb // ns, nt),
        in_specs=in_specs,
        out_specs=out_specs,
        out_shape=out_shape,
        scratch_shapes=[
            pltpu.VMEM((ns, HALO_A + tl, d), F32),
            pltpu.VMEM((ns, (HALO_B + tl) * NBLK, LANES), F32),
            pltpu.VMEM((ns, tl * NBLK, LANES), F32),
        ],
        compiler_params=pltpu.CompilerParams(
            dimension_semantics=("arbitrary", "arbitrary"),
            vmem_limit_bytes=VMEM_LIMIT_BYTES),
        name=f"mixer_tl{tl}",
    )(x, sa, sb, *weights)


def _moe_kernel(tb, ch,
                cnt_ref,
                h1_ref, info_ref, p_ref, gmoe_ref, wgu_ref, wd_ref,
                gple_ref, wpg_ref, wpp_ref, gfin_ref,
                out_ref,
                xn_scr, col_scr, row_scr, comb_scr, act_scr, y_scr):
    d = D_MODEL
    i = pl.program_id(0)
    g = pl.program_id(1)
    chunks = [(cnt_ref[i * N_GROUPS + k] + ch - 1) // ch for k in range(N_GROUPS)]
    starts = [sum(chunks[:k]) * ch for k in range(N_GROUPS)]

    @pl.when((i == 0) & (g == 0))
    def _():
        y_scr[...] = jnp.zeros(y_scr.shape, BF16)

    @pl.when(g == 0)
    def _():
        xn_scr[...] = _rms(h1_ref[...], gmoe_ref[...]).astype(BF16)
        info = info_ref[...]
        lane = lax.broadcasted_iota(jnp.int32, (tb, LANES), 1).astype(F32)
        onehot = jnp.where(lane < N_GROUPS, info, 0.0)
        rb = min(tb, OUT_ROWS)
        tok = lax.broadcasted_iota(jnp.int32, (rb, tb), 1)
        row = lax.broadcasted_iota(jnp.int32, (rb, tb), 0)
        oh16 = onehot.astype(BF16)
        before = jnp.concatenate(
            [_dot(jnp.where(tok < row + r * rb, 1.0, 0.0).astype(BF16), oh16)
             for r in range(tb // rb)], axis=0)
        mypos = jnp.sum(onehot * before, axis=-1, keepdims=True)
        mygrp = jnp.sum(onehot * lane, axis=-1, keepdims=True)
        slot = mypos
        for k in range(1, N_GROUPS):
            slot = slot + jnp.where(mygrp == float(k), starts[k].astype(F32), 0.0)
        col = jnp.where(lane == 0.0, mygrp,
                        jnp.where(lane == 1.0, mypos, jnp.where(lane == 2.0, slot, 0.0)))
        col_scr[...] = col
        row_scr[...] = col.T
        hi = info.astype(BF16)
        comb_scr[:, 0:LANES] = hi
        comb_scr[:, LANES:2 * LANES] = (info - hi.astype(F32)).astype(BF16)

    n_chunks = chunks[0]
    start = starts[0]
    for k in range(1, N_GROUPS):
        n_chunks = jnp.where(g == k, chunks[k], n_chunks)
        start = jnp.where(g == k, starts[k], start)
    gf = g.astype(F32)
    rsel = jnp.where(row_scr[0:1, :] == gf, row_scr[1:2, :], -1.0)
    sub_iota = lax.broadcasted_iota(jnp.int32, (ch, tb), 0).astype(F32)

    def chunk_body(c, carry):
        base = (c * ch).astype(F32)
        sel = jnp.where(rsel == sub_iota + base, 1.0, 0.0).astype(BF16)
        xs = _dot(sel, xn_scr[...]).astype(BF16)
        cw2 = _dot(sel, comb_scr[...])
        cw = cw2[:, 0:LANES] + cw2[:, LANES:2 * LANES]
        for e in range(EXPERTS_PER_GROUP):
            hgu = _dot(xs, wgu_ref[e])
            gate = hgu[:, 0:D_EXPERT]
            up = hgu[:, D_EXPERT:2 * D_EXPERT]
            a = gate * _sigmoid(gate) * up * cw[:, INFO_COMB_LANE + e:INFO_COMB_LANE + e + 1]
            act_scr[:, e * D_EXPERT:(e + 1) * D_EXPERT] = a.astype(BF16)
        y = _dot(act_scr[...], wd_ref[...].reshape(EXPERTS_PER_GROUP * D_EXPERT, d))
        y_scr[pl.ds(pl.multiple_of(start + c * ch, ch), ch), :] = y.astype(BF16)
        return carry

    lax.fori_loop(0, n_chunks, chunk_body, 0)

    @pl.when(g == N_GROUPS - 1)
    def _():
        yrows = y_scr.shape[0]
        rb = min(tb, OUT_ROWS)
        lane_iota = lax.broadcasted_iota(jnp.int32, (rb, yrows), 1).astype(F32)
        for r in range(tb // rb):
            rows = slice(r * rb, (r + 1) * rb)
            selt = jnp.where(col_scr[rows, 2:3] == lane_iota, 1.0, 0.0).astype(BF16)
            h2 = h1_ref[rows, :] + _dot(selt, y_scr[...])
            gate = _sigmoid(_dot(_rms(h2, gple_ref[...]).astype(BF16), wpg_ref[...]))
            h3 = h2 + gate * _dot(p_ref[rows, :].astype(BF16), wpp_ref[...])
            out_ref[rows, :] = _rms(h3, gfin_ref[...])


def _moe(h1, info, p, cnt, w, tb, ch):
    n, d = h1.shape
    ntb = n // tb
    kern = functools.partial(_moe_kernel, tb, ch)
    yrows = -(-(tb + N_GROUPS * ch) // MXU_K) * MXU_K
    epg = EXPERTS_PER_GROUP
    grid_spec = pltpu.PrefetchScalarGridSpec(
        num_scalar_prefetch=1,
        grid=(ntb, N_GROUPS),
        in_specs=[
            pl.BlockSpec((tb, d), lambda i, g, c: (i, 0)),
            pl.BlockSpec((tb, LANES), lambda i, g, c: (i, 0)),
            pl.BlockSpec((tb, D_PLE), lambda i, g, c: (i, 0)),
            pl.BlockSpec((1, d), lambda i, g, c: (0, 0)),
            pl.BlockSpec((epg, d, 2 * D_EXPERT), lambda i, g, c: (g, 0, 0)),
            pl.BlockSpec((epg, D_EXPERT, d), lambda i, g, c: (g, 0, 0)),
            pl.BlockSpec((1, d), lambda i, g, c: (0, 0)),
            pl.BlockSpec((d, d), lambda i, g, c: (0, 0)),
            pl.BlockSpec((D_PLE, d), lambda i, g, c: (0, 0)),
            pl.BlockSpec((1, d), lambda i, g, c: (0, 0)),
        ],
        out_specs=pl.BlockSpec((tb, d), lambda i, g, c: (i, 0)),
        scratch_shapes=[
            pltpu.VMEM((tb, d), BF16),
            pltpu.VMEM((tb, LANES), F32),
            pltpu.VMEM((LANES, tb), F32),
            pltpu.VMEM((tb, 2 * LANES), BF16),
            pltpu.VMEM((ch, epg * D_EXPERT), BF16),
            pltpu.VMEM((yrows, d), BF16),
        ],
    )
    return pl.pallas_call(
        kern,
        grid_spec=grid_spec,
        out_shape=jax.ShapeDtypeStruct((n, d), F32),
        compiler_params=pltpu.CompilerParams(
            dimension_semantics=("arbitrary", "arbitrary"),
            vmem_limit_bytes=VMEM_LIMIT_BYTES),
        name=f"moe_tb{tb}",
    )(cnt, h1, info, p, w["g_moe"], w["w_gate_up"], w["w_down"],
      w["g_ple"], w["w_ple_gate"], w["w_ple_proj"], w["g_final"])


def _tiles(nb, l):
    n = nb * l
    tl = min(l, 256)
    tb = min(n, 1024)
    return tl, tb, min(tb, 128)


def _layer(x, p, sa, sb, w):
    nb, l, d = x.shape
    tl, tb, ch = _tiles(nb, l)
    h1, info, cnt, na, nbuf = _mixer(x, sa, sb, w, tl)
    n = nb * l
    cnt = cnt[:, :, 0, :N_GROUPS].reshape(n // tb, tb // tl, N_GROUPS).sum(axis=1).reshape(-1)
    y = _moe(h1.reshape(n, d), info.reshape(n, LANES), p.reshape(n, D_PLE), cnt, w, tb, ch)
    return y.reshape(nb, l, d), na, nbuf


def kernel(x_prompt, x_sample, p_prompt, p_sample, state_conv_a, state_conv_b, g_mix, w_in, b_in,
           conv_a_w, w_a_out, conv_b_w, conv_b_b, ln_b_g, ln_b_b, w_b_out, b_b_out, w_o, g_moe,
           w_group, b_group, w_erouter, b_erouter, w_gate_up, w_down, g_ple, w_ple_gate,
           w_ple_proj, g_final):
    depth = g_mix.shape[0]
    assert depth == 1, "the moe kernel fuses the final norm, so only a one-layer trunk is supported"
    hp, hs = x_prompt, x_sample
    nb = x_prompt.shape[0]
    a_p, b_p, a_s, b_s = [], [], [], []
    row = lambda v: v.reshape(1, -1).astype(F32)
    for li in range(depth):
        pad = LANES - N_GROUPS - N_GROUPS * EXPERTS_PER_GROUP
        w_router = jnp.concatenate(
            [w_group[li], w_erouter[li], jnp.zeros((D_MODEL, pad), F32)], axis=1).astype(BF16)
        b_router = jnp.concatenate([b_group[li], b_erouter[li], jnp.zeros((pad,), F32)]).reshape(1, LANES)
        d = D_MODEL
        win, bi = w_in[li].astype(BF16), b_in[li]
        z_cols = lambda a: jnp.concatenate([a[..., 0:3 * d], a[..., 5 * d:6 * d]], axis=-1)
        w = dict(
            g_mix=row(g_mix[li]),
            w_vg=win[:, 3 * d:5 * d], b_vg=row(bi[3 * d:5 * d]),
            w_z=z_cols(win).reshape(d, Z_SEGS, d).transpose(1, 0, 2),
            b_z=z_cols(bi).reshape(Z_SEGS, 1, d),
            w_gb=win[:, 6 * d:7 * d], b_gb=row(bi[6 * d:7 * d]),
            conv_a_w=conv_a_w[li], w_a_out=w_a_out[li].astype(BF16),
            conv_b_w=conv_b_w[li].reshape(K_B * NBLK, LANES),
            conv_b_b=conv_b_b[li].reshape(NBLK, LANES), ln_b_g=row(ln_b_g[li]), ln_b_b=row(ln_b_b[li]),
            w_b_out=w_b_out[li].astype(BF16), b_b_out=row(b_b_out[li]), w_o=w_o[li].astype(BF16),
            g_moe=row(g_moe[li]), w_router=w_router, b_router=b_router,
            w_gate_up=w_gate_up[li].astype(BF16), w_down=w_down[li].astype(BF16),
            g_ple=row(g_ple[li]), w_ple_gate=w_ple_gate[li].astype(BF16),
            w_ple_proj=w_ple_proj[li].astype(BF16), g_final=row(g_final))
        zero_a = jnp.zeros((nb, K_A - 1, D_MODEL), F32)
        zero_b = jnp.zeros((nb, K_B - 1, D_MODEL), F32)
        hp, na_p, nb_p = _layer(hp, p_prompt[li], zero_a, zero_b, w)
        hs, na_s, nb_s = _layer(hs, p_sample[li], state_conv_a[li], state_conv_b[li], w)
        a_p.append(na_p)
        b_p.append(nb_p)
        a_s.append(na_s)
        b_s.append(nb_s)
    return (hp, hs, jnp.stack(a_p), jnp.stack(b_p), jnp.stack(a_s), jnp.stack(b_s))
```

```python
import functools

import jax
import jax.numpy as jnp
from jax import lax
from jax.experimental import pallas as pl
from jax.experimental.pallas import tpu as pltpu

D_MODEL = 1024
D_PLE = 256
K_A = 3
K_B = 31
N_GROUPS = 4
EXPERTS_PER_GROUP = 8
D_EXPERT = 256
EPS = 1e-6

MXU_K = 256
LANES = 128
SUBLANES = 8
HALO_A = 8
HALO_B = 32
NBLK = D_MODEL // LANES
CONV_STEPS = 8
CONV_UNROLL = 2
Z_SEGS = 4
N_STREAMS = 2
PAIR_CHUNKS = (144, 160)
BF16_ROWS = 16
OUT_ROWS = 256
INFO_GROUP_LANE = 0
INFO_COMB_LANE = 8
VMEM_LIMIT_BYTES = 60 * 1024 * 1024

F32 = jnp.float32
BF16 = jnp.bfloat16
NEG_BIG = -1e30
NEG_LOG2_E = -1.4426950408889634


def _rms(x, g):
    ms = jnp.mean(x * x, axis=-1, keepdims=True)
    return x * lax.rsqrt(ms + EPS) * g


def _sigmoid(x):
    return 1.0 / (1.0 + jnp.exp2(x * NEG_LOG2_E))


def _dot(a, b):
    return jnp.dot(a, b, preferred_element_type=F32)


def _rows_to_time_major(dst_ref, t0, val):
    r = val.shape[0]
    for j in range(NBLK):
        dst_ref[pl.ds(t0 * NBLK + j, r, stride=NBLK), :] = val[:, j * LANES:(j + 1) * LANES]


def _ds(start, size):
    if isinstance(start, int):
        return pl.ds(start, size)
    return pl.ds(pl.multiple_of(start, size), size)


def _time_major_to_rows(src_ref, t0, r):
    return jnp.concatenate(
        [src_ref[pl.ds(t0 * NBLK + j, r, stride=NBLK), :] for j in range(NBLK)], axis=-1)


def _route(logits):
    r = logits.shape[0]
    lane = lax.broadcasted_iota(jnp.int32, (r, LANES), 1).astype(F32)
    is_g = lane < N_GROUPS
    gl = jnp.where(is_g, logits, NEG_BIG)
    gmax = jnp.max(gl, axis=-1, keepdims=True)
    gidx = jnp.min(jnp.where(gl == gmax, lane, float(LANES)), axis=-1, keepdims=True)
    gsum = jnp.sum(jnp.where(is_g, jnp.exp(gl - gmax), 0.0), axis=-1, keepdims=True)
    p_g = 1.0 / gsum
    lo = float(N_GROUPS) + gidx * float(EXPERTS_PER_GROUP)
    in_grp = (lane >= lo) & (lane < lo + float(EXPERTS_PER_GROUP))
    el = jnp.where(in_grp, logits, NEG_BIG)
    m1 = jnp.max(el, axis=-1, keepdims=True)
    i1 = jnp.min(jnp.where(el == m1, lane, float(LANES)), axis=-1, keepdims=True)
    el2 = jnp.where(lane == i1, NEG_BIG, el)
    m2 = jnp.max(el2, axis=-1, keepdims=True)
    i2 = jnp.min(jnp.where(el2 == m2, lane, float(LANES)), axis=-1, keepdims=True)
    e2 = jnp.exp(m2 - m1)
    w1 = 1.0 / (1.0 + e2)
    w2 = e2 * w1
    shift = float(INFO_COMB_LANE) - lo
    onehot = jnp.where(lane == gidx + float(INFO_GROUP_LANE), 1.0, 0.0)
    comb = (jnp.where(lane == i1 + shift, p_g * w1, 0.0)
            + jnp.where(lane == i2 + shift, p_g * w2, 0.0))
    return onehot, comb


def _mixer_kernel(tl, nt, ns,
                  x_ref, sa_ref, sb_ref, gmix_ref, wvg_ref, bvg_ref, wz_ref, bz_ref, wgb_ref, bgb_ref,
                  caw_ref, waout_ref, cbw_ref, cbb_ref, lng_ref, lnb_ref, wbout_ref, bbout_ref,
                  wo_ref, gmoe_ref, wr_ref, br_ref,
                  h1_ref, info_ref, cnt_ref, na_ref, nb_ref,
                  cx_scr, glu_scr, conv_scr):
    d = D_MODEL
    i = pl.program_id(1)
    streams = range(ns)

    @pl.when(i == 0)
    def _():
        for s in streams:
            cx_scr[s, 0:HALO_A, :] = jnp.zeros((HALO_A, d), F32)
            cx_scr[s, HALO_A - (K_A - 1):HALO_A, :] = sa_ref[s]
            glu_scr[s, 0:HALO_B * NBLK, :] = jnp.zeros((HALO_B * NBLK, LANES), F32)
            _rows_to_time_major(glu_scr.at[s], HALO_B - (K_B - 1), sb_ref[s])

    @pl.when(i > 0)
    def _():
        for s in streams:
            cx_scr[s, 0:HALO_A, :] = cx_scr[s, tl:tl + HALO_A, :]
            glu_scr[s, 0:HALO_B * NBLK, :] = glu_scr[s, tl * NBLK:(tl + HALO_B) * NBLK, :]

    x = [x_ref[s] for s in streams]
    n = [_rms(x[s], gmix_ref[...]).astype(BF16) for s in streams]

    for s in streams:
        vg = _dot(n[s], wvg_ref[...]) + bvg_ref[...]
        glu = vg[:, 0:d] * _sigmoid(vg[:, d:2 * d])
        _rows_to_time_major(glu_scr.at[s], HALO_B, glu)

    base_b = HALO_B - (K_B - 1)
    bias = cbb_ref[...]
    group = min(CONV_STEPS, tl)

    def conv_body(c, carry):
        for s in streams:
            t0 = c * group
            accs = [bias] * group
            for j in range(K_B + group - 1):
                xj = glu_scr[s, _ds((t0 + base_b + j) * NBLK, NBLK), :]
                for u in range(group):
                    k = j - u
                    if 0 <= k < K_B:
                        accs[u] = accs[u] + cbw_ref[k * NBLK:(k + 1) * NBLK, :] * xj
            for u in range(group):
                conv_scr[s, _ds((t0 + u) * NBLK, NBLK), :] = accs[u]
        return carry

    lax.fori_loop(0, tl // group, conv_body, 0, unroll=CONV_UNROLL)

    def zseg(s, k):
        return _dot(n[s], wz_ref[k]) + bz_ref[k]

    ya = []
    for s in streams:
        cx = zseg(s, 1) * zseg(s, 2)
        cx_scr[s, HALO_A:HALO_A + tl, :] = cx
        base_a = HALO_A - (K_A - 1)
        conv_a = caw_ref[K_A - 1:K_A, :] * cx
        for k in range(K_A - 1):
            conv_a = conv_a + caw_ref[k:k + 1, :] * cx_scr[s, base_a + k:base_a + k + tl, :]
        ya.append(_dot((zseg(s, 0) * conv_a).astype(BF16), waout_ref[...]))

    yb = []
    for s in streams:
        cbv = _time_major_to_rows(conv_scr.at[s], 0, tl)
        mu = jnp.mean(cbv, axis=-1, keepdims=True)
        xc = cbv - mu
        var = jnp.mean(xc * xc, axis=-1, keepdims=True)
        ln = xc * lax.rsqrt(var + EPS) * lng_ref[...] + lnb_ref[...]
        cbact = (ln * _sigmoid(ln)).astype(BF16)
        yb.append(_dot(cbact, wbout_ref[...]) + bbout_ref[...])

    h1 = []
    for s in streams:
        gate_b = _sigmoid(_dot(n[s], wgb_ref[...]) + bgb_ref[...])
        merged = _sigmoid(zseg(s, 3)) * ya[s] + gate_b * yb[s]
        h1.append(x[s] + _dot(merged.astype(BF16), wo_ref[...]))
        h1_ref[s] = h1[s]

    @pl.when(i == nt - 1)
    def _():
        for s in streams:
            na_ref[s] = cx_scr[s, tl + HALO_A - (K_A - 1):tl + HALO_A, :]
            nb_ref[s] = _time_major_to_rows(glu_scr.at[s], tl + HALO_B - (K_B - 1), K_B - 1)

    for s in streams:
        xn = _rms(h1[s], gmoe_ref[...]).astype(BF16)
        logits = _dot(xn, wr_ref[...]) + br_ref[...]
        onehot, comb = _route(logits)
        info_ref[s] = onehot + comb
        counts = jnp.sum(onehot, axis=0, keepdims=True)
        cnt_ref[s, 0] = jnp.broadcast_to(counts, (SUBLANES, LANES)).astype(jnp.int32)


def _const_spec(shape):
    zeros = (0,) * len(shape)
    return pl.BlockSpec(shape, lambda b, i: zeros)


def _mixer(x, sa, sb, w, tl):
    nb, l, d = x.shape
    nt = l // tl
    ns = N_STREAMS
    kern = functools.partial(_mixer_kernel, tl, nt, ns)
    weights = [w["g_mix"], w["w_vg"], w["b_vg"], w["w_z"], w["b_z"], w["w_gb"], w["b_gb"],
               w["conv_a_w"], w["w_a_out"], w["conv_b_w"], w["conv_b_b"], w["ln_b_g"], w["ln_b_b"],
               w["w_b_out"], w["b_b_out"], w["w_o"], w["g_moe"], w["w_router"], w["b_router"]]
    in_specs = [
        pl.BlockSpec((ns, tl, d), lambda b, i: (b, i, 0)),
        pl.BlockSpec((ns, K_A - 1, d), lambda b, i: (b, 0, 0)),
        pl.BlockSpec((ns, K_B - 1, d), lambda b, i: (b, 0, 0)),
    ] + [_const_spec(a.shape) for a in weights]
    out_shape = [
        jax.ShapeDtypeStruct((nb, l, d), F32),
        jax.ShapeDtypeStruct((nb, l, LANES), F32),
        jax.ShapeDtypeStruct((nb, nt, SUBLANES, LANES), jnp.int32),
        jax.ShapeDtypeStruct((nb, K_A - 1, d), F32),
        jax.ShapeDtypeStruct((nb, K_B - 1, d), F32),
    ]
    out_specs = [
        pl.BlockSpec((ns, tl, d), lambda b, i: (b, i, 0)),
        pl.BlockSpec((ns, tl, LANES), lambda b, i: (b, i, 0)),
        pl.BlockSpec((ns, 1, SUBLANES, LANES), lambda b, i: (b, i, 0, 0)),
        pl.BlockSpec((ns, K_A - 1, d), lambda b, i: (b, 0, 0)),
        pl.BlockSpec((ns, K_B - 1, d), lambda b, i: (b, 0, 0)),
    ]
    return pl.pallas_call(
        kern,
        grid=(nb // ns, nt),
        in_specs=in_specs,
        out_specs=out_specs,
        out_shape=out_shape,
        scratch_shapes=[
            pltpu.VMEM((ns, HALO_A + tl, d), F32),
            pltpu.VMEM((ns, (HALO_B + tl) * NBLK, LANES), F32),
            pltpu.VMEM((ns, tl * NBLK, LANES), F32),
        ],
        compiler_params=pltpu.CompilerParams(
            dimension_semantics=("arbitrary", "arbitrary"),
            vmem_limit_bytes=VMEM_LIMIT_BYTES),
        name=f"mixer_tl{tl}",
    )(x, sa, sb, *weights)


def _moe_kernel(tb, classes,
                cnt_ref,
                h1_ref, info_ref, p_ref, gmoe_ref, wgu_ref, wd_ref,
                gple_ref, wpg_ref, wpp_ref, gfin_ref,
                out_ref,
                xn_scr, col_scr, row_scr, comb_scr, act_scr, y_scr):
    d = D_MODEL
    i = pl.program_id(0)
    g = pl.program_id(1)
    sizes, chunks = [], []
    for k in range(N_GROUPS):
        cnt = cnt_ref[i * N_GROUPS + k]
        size, num = jnp.int32(classes[0]), (cnt + classes[0] - 1) // classes[0]
        for small, big in zip(classes[:-1], classes[1:]):
            fits = (cnt > 2 * small) & (cnt <= 2 * big)
            size = jnp.where(fits, big, size)
            num = jnp.where(fits, 2, num)
        sizes.append(size)
        chunks.append(num)
    starts = [sum([sizes[j] * chunks[j] for j in range(k)], jnp.int32(0)) for k in range(N_GROUPS + 1)]

    @pl.when((i == 0) & (g == 0))
    def _():
        y_scr[...] = jnp.zeros(y_scr.shape, BF16)

    @pl.when(g == 0)
    def _():
        xn_scr[...] = _rms(h1_ref[...], gmoe_ref[...]).astype(BF16)
        info = info_ref[...]
        lane = lax.broadcasted_iota(jnp.int32, (tb, LANES), 1).astype(F32)
        onehot = jnp.where(lane < N_GROUPS, info, 0.0)
        rb = min(tb, OUT_ROWS)
        tok = lax.broadcasted_iota(jnp.int32, (rb, tb), 1)
        row = lax.broadcasted_iota(jnp.int32, (rb, tb), 0)
        oh16 = onehot.astype(BF16)
        before = jnp.concatenate(
            [_dot(jnp.where(tok < row + r * rb, 1.0, 0.0).astype(BF16), oh16)
             for r in range(tb // rb)], axis=0)
        mypos = jnp.sum(onehot * before, axis=-1, keepdims=True)
        mygrp = jnp.sum(onehot * lane, axis=-1, keepdims=True)
        slot = mypos
        for k in range(1, N_GROUPS):
            slot = slot + jnp.where(mygrp == float(k), starts[k].astype(F32), 0.0)
        col = jnp.where(lane == 0.0, mygrp,
                        jnp.where(lane == 1.0, mypos, jnp.where(lane == 2.0, slot, 0.0)))
        col_scr[...] = col
        row_scr[...] = col.T
        hi = info.astype(BF16)
        comb_scr[:, 0:LANES] = hi
        comb_scr[:, LANES:2 * LANES] = (info - hi.astype(F32)).astype(BF16)

    size, n_chunks, start = sizes[0], chunks[0], starts[0]
    for k in range(1, N_GROUPS):
        size = jnp.where(g == k, sizes[k], size)
        n_chunks = jnp.where(g == k, chunks[k], n_chunks)
        start = jnp.where(g == k, starts[k], start)
    gf = g.astype(F32)
    rsel = jnp.where(row_scr[0:1, :] == gf, row_scr[1:2, :], -1.0)

    def run_chunks(ch):
        sub_iota = lax.broadcasted_iota(jnp.int32, (ch, tb), 0).astype(F32)

        def chunk_body(c, carry):
            base = (c * ch).astype(F32)
            sel = jnp.where(rsel == sub_iota + base, 1.0, 0.0).astype(BF16)
            xs = _dot(sel, xn_scr[...]).astype(BF16)
            cw2 = _dot(sel, comb_scr[...])
            cw = cw2[:, 0:LANES] + cw2[:, LANES:2 * LANES]
            for e in range(EXPERTS_PER_GROUP):
                hgu = _dot(xs, wgu_ref[e])
                gate = hgu[:, 0:D_EXPERT]
                up = hgu[:, D_EXPERT:2 * D_EXPERT]
                a = gate * _sigmoid(gate) * up * cw[:, INFO_COMB_LANE + e:INFO_COMB_LANE + e + 1]
                act_scr[0:ch, e * D_EXPERT:(e + 1) * D_EXPERT] = a.astype(BF16)
            y = _dot(act_scr[0:ch, :], wd_ref[...].reshape(EXPERTS_PER_GROUP * D_EXPERT, d))
            row0 = pl.multiple_of(start + c * ch, BF16_ROWS)
            y_scr[pl.ds(row0, ch), :] = y.astype(BF16)
            return carry

        lax.fori_loop(0, n_chunks, chunk_body, 0)

    for ch in classes:
        pl.when(size == ch)(functools.partial(run_chunks, ch))

    def epilogue(yrows):
        rb = min(tb, OUT_ROWS)
        lane_iota = lax.broadcasted_iota(jnp.int32, (rb, yrows), 1).astype(F32)
        for r in range(tb // rb):
            rows = slice(r * rb, (r + 1) * rb)
            selt = jnp.where(col_scr[rows, 2:3] == lane_iota, 1.0, 0.0).astype(BF16)
            h2 = h1_ref[rows, :] + _dot(selt, y_scr[0:yrows, :])
            gate = _sigmoid(_dot(_rms(h2, gple_ref[...]).astype(BF16), wpg_ref[...]))
            h3 = h2 + gate * _dot(p_ref[rows, :].astype(BF16), wpp_ref[...])
            out_ref[rows, :] = _rms(h3, gfin_ref[...])

    last = g == N_GROUPS - 1
    yfull = y_scr.shape[0]
    yshort = yfull - MXU_K
    if yshort >= tb:
        pl.when(last & (starts[N_GROUPS] <= yshort))(functools.partial(epilogue, yshort))
        pl.when(last & (starts[N_GROUPS] > yshort))(functools.partial(epilogue, yfull))
    else:
        pl.when(last)(functools.partial(epilogue, yfull))


def _moe(h1, info, p, cnt, w, tb, ch):
    n, d = h1.shape
    ntb = n // tb
    classes = (ch,) + tuple(c for c in PAIR_CHUNKS if c > ch and 2 * c <= tb)
    kern = functools.partial(_moe_kernel, tb, classes)
    yrows = -(-(tb + N_GROUPS * ch) // MXU_K) * MXU_K
    epg = EXPERTS_PER_GROUP
    grid_spec = pltpu.PrefetchScalarGridSpec(
        num_scalar_prefetch=1,
        grid=(ntb, N_GROUPS),
        in_specs=[
            pl.BlockSpec((tb, d), lambda i, g, c: (i, 0)),
            pl.BlockSpec((tb, LANES), lambda i, g, c: (i, 0)),
            pl.BlockSpec((tb, D_PLE), lambda i, g, c: (i, 0)),
            pl.BlockSpec((1, d), lambda i, g, c: (0, 0)),
            pl.BlockSpec((epg, d, 2 * D_EXPERT), lambda i, g, c: (g, 0, 0)),
            pl.BlockSpec((epg, D_EXPERT, d), lambda i, g, c: (g, 0, 0)),
            pl.BlockSpec((1, d), lambda i, g, c: (0, 0)),
            pl.BlockSpec((d, d), lambda i, g, c: (0, 0)),
            pl.BlockSpec((D_PLE, d), lambda i, g, c: (0, 0)),
            pl.BlockSpec((1, d), lambda i, g, c: (0, 0)),
        ],
        out_specs=pl.BlockSpec((tb, d), lambda i, g, c: (i, 0)),
        scratch_shapes=[
            pltpu.VMEM((tb, d), BF16),
            pltpu.VMEM((tb, LANES), F32),
            pltpu.VMEM((LANES, tb), F32),
            pltpu.VMEM((tb, 2 * LANES), BF16),
            pltpu.VMEM((max(classes), epg * D_EXPERT), BF16),
            pltpu.VMEM((yrows, d), BF16),
        ],
    )
    return pl.pallas_call(
        kern,
        grid_spec=grid_spec,
        out_shape=jax.ShapeDtypeStruct((n, d), F32),
        compiler_params=pltpu.CompilerParams(
            dimension_semantics=("arbitrary", "arbitrary"),
            vmem_limit_bytes=VMEM_LIMIT_BYTES),
        name=f"moe_tb{tb}",
    )(cnt, h1, info, p, w["g_moe"], w["w_gate_up"], w["w_down"],
      w["g_ple"], w["w_ple_gate"], w["w_ple_proj"], w["g_final"])


def _tiles(nb, l):
    n = nb * l
    tl = min(l, 256)
    tb = min(n, 1024)
    return tl, tb, min(tb, 128)


def _layer(x, p, sa, sb, w):
    nb, l, d = x.shape
    tl, tb, ch = _tiles(nb, l)
    h1, info, cnt, na, nbuf = _mixer(x, sa, sb, w, tl)
    n = nb * l
    cnt = cnt[:, :, 0, :N_GROUPS].reshape(n // tb, tb // tl, N_GROUPS).sum(axis=1).reshape(-1)
    y = _moe(h1.reshape(n, d), info.reshape(n, LANES), p.reshape(n, D_PLE), cnt, w, tb, ch)
    return y.reshape(nb, l, d), na, nbuf


def kernel(x_prompt, x_sample, p_prompt, p_sample, state_conv_a, state_conv_b, g_mix, w_in, b_in,
           conv_a_w, w_a_out, conv_b_w, conv_b_b, ln_b_g, ln_b_b, w_b_out, b_b_out, w_o, g_moe,
           w_group, b_group, w_erouter, b_erouter, w_gate_up, w_down, g_ple, w_ple_gate,
           w_ple_proj, g_final):
    depth = g_mix.shape[0]
    assert depth == 1, "the moe kernel fuses the final norm, so only a one-layer trunk is supported"
    hp, hs = x_prompt, x_sample
    nb = x_prompt.shape[0]
    a_p, b_p, a_s, b_s = [], [], [], []
    row = lambda v: v.reshape(1, -1).astype(F32)
    for li in range(depth):
        pad = LANES - N_GROUPS - N_GROUPS * EXPERTS_PER_GROUP
        w_router = jnp.concatenate(
            [w_group[li], w_erouter[li], jnp.zeros((D_MODEL, pad), F32)], axis=1).astype(BF16)
        b_router = jnp.concatenate([b_group[li], b_erouter[li], jnp.zeros((pad,), F32)]).reshape(1, LANES)
        d = D_MODEL
        win, bi = w_in[li].astype(BF16), b_in[li]
        z_cols = lambda a: jnp.concatenate([a[..., 0:3 * d], a[..., 5 * d:6 * d]], axis=-1)
        w = dict(
            g_mix=row(g_mix[li]),
            w_vg=win[:, 3 * d:5 * d], b_vg=row(bi[3 * d:5 * d]),
            w_z=z_cols(win).reshape(d, Z_SEGS, d).transpose(1, 0, 2),
            b_z=z_cols(bi).reshape(Z_SEGS, 1, d),
            w_gb=win[:, 6 * d:7 * d], b_gb=row(bi[6 * d:7 * d]),
            conv_a_w=conv_a_w[li], w_a_out=w_a_out[li].astype(BF16),
            conv_b_w=conv_b_w[li].reshape(K_B * NBLK, LANES),
            conv_b_b=conv_b_b[li].reshape(NBLK, LANES), ln_b_g=row(ln_b_g[li]), ln_b_b=row(ln_b_b[li]),
            w_b_out=w_b_out[li].astype(BF16), b_b_out=row(b_b_out[li]), w_o=w_o[li].astype(BF16),
            g_moe=row(g_moe[li]), w_router=w_router, b_router=b_router,
            w_gate_up=w_gate_up[li].astype(BF16), w_down=w_down[li].astype(BF16),
            g_ple=row(g_ple[li]), w_ple_gate=w_ple_gate[li].astype(BF16),
            w_ple_proj=w_ple_proj[li].astype(BF16), g_final=row(g_final))
        zero_a = jnp.zeros((nb, K_A - 1, D_MODEL), F32)
        zero_b = jnp.zeros((nb, K_B - 1, D_MODEL), F32)
        hp, na_p, nb_p = _layer(hp, p_prompt[li], zero_a, zero_b, w)
        hs, na_s, nb_s = _layer(hs, p_sample[li], state_conv_a[li], state_conv_b[li], w)
        a_p.append(na_p)
        b_p.append(nb_p)
        a_s.append(na_s)
        b_s.append(nb_s)
    return (hp, hs, jnp.stack(a_p), jnp.stack(b_p), jnp.stack(a_s), jnp.stack(b_s))
```

```python
import functools

import jax
import jax.numpy as jnp
from jax import lax
from jax.experimental import pallas as pl
from jax.experimental.pallas import tpu as pltpu

D_MODEL = 1024
D_PLE = 256
K_A = 3
K_B = 31
N_GROUPS = 4
EXPERTS_PER_GROUP = 8
D_EXPERT = 256
EPS = 1e-6

MXU_K = 256
MXU_N = 256
LANES = 128
SUBLANES = 8
HALO_A = 8
HALO_B = 32
NBLK = D_MODEL // LANES
CONV_STEPS = 8
CONV_UNROLL = 2
Z_SEGS = 4
N_STREAMS = 2
PAIR_CHUNKS = (144, 160)
BF16_ROWS = 16
OUT_ROWS = 256
INFO_GROUP_LANE = 0
INFO_COMB_LANE = 8
VMEM_LIMIT_BYTES = 60 * 1024 * 1024

F32 = jnp.float32
BF16 = jnp.bfloat16
NEG_BIG = -1e30
NEG_LOG2_E = -1.4426950408889634


def _rms(x, g):
    ms = jnp.mean(x * x, axis=-1, keepdims=True)
    return x * lax.rsqrt(ms + EPS) * g


def _sigmoid(x):
    return 1.0 / (1.0 + jnp.exp2(x * NEG_LOG2_E))


def _dot(a, b):
    return jnp.dot(a, b, preferred_element_type=F32)


def _col_tiles(w):
    k, n = w.shape
    return w.reshape(k, n // MXU_N, MXU_N).transpose(1, 0, 2)


def _dot_tiled(a, w_ref):
    return jnp.concatenate([_dot(a, w_ref[q]) for q in range(w_ref.shape[0])], axis=-1)


def _rows_to_time_major(dst_ref, t0, val):
    r = val.shape[0]
    for j in range(NBLK):
        dst_ref[pl.ds(t0 * NBLK + j, r, stride=NBLK), :] = val[:, j * LANES:(j + 1) * LANES]


def _ds(start, size):
    if isinstance(start, int):
        return pl.ds(start, size)
    return pl.ds(pl.multiple_of(start, size), size)


def _time_major_to_rows(src_ref, t0, r):
    return jnp.concatenate(
        [src_ref[pl.ds(t0 * NBLK + j, r, stride=NBLK), :] for j in range(NBLK)], axis=-1)


def _route(logits):
    r = logits.shape[0]
    lane = lax.broadcasted_iota(jnp.int32, (r, LANES), 1).astype(F32)
    is_g = lane < N_GROUPS
    gl = jnp.where(is_g, logits, NEG_BIG)
    gmax = jnp.max(gl, axis=-1, keepdims=True)
    gidx = jnp.min(jnp.where(gl == gmax, lane, float(LANES)), axis=-1, keepdims=True)
    gsum = jnp.sum(jnp.where(is_g, jnp.exp(gl - gmax), 0.0), axis=-1, keepdims=True)
    p_g = 1.0 / gsum
    lo = float(N_GROUPS) + gidx * float(EXPERTS_PER_GROUP)
    in_grp = (lane >= lo) & (lane < lo + float(EXPERTS_PER_GROUP))
    el = jnp.where(in_grp, logits, NEG_BIG)
    m1 = jnp.max(el, axis=-1, keepdims=True)
    i1 = jnp.min(jnp.where(el == m1, lane, float(LANES)), axis=-1, keepdims=True)
    el2 = jnp.where(lane == i1, NEG_BIG, el)
    m2 = jnp.max(el2, axis=-1, keepdims=True)
    i2 = jnp.min(jnp.where(el2 == m2, lane, float(LANES)), axis=-1, keepdims=True)
    e2 = jnp.exp(m2 - m1)
    w1 = 1.0 / (1.0 + e2)
    w2 = e2 * w1
    shift = float(INFO_COMB_LANE) - lo
    onehot = jnp.where(lane == gidx + float(INFO_GROUP_LANE), 1.0, 0.0)
    comb = (jnp.where(lane == i1 + shift, p_g * w1, 0.0)
            + jnp.where(lane == i2 + shift, p_g * w2, 0.0))
    return onehot, comb


def _mixer_kernel(tl, nt, ns,
                  x_ref, sa_ref, sb_ref, gmix_ref, wvg_ref, bvg_ref, wz_ref, bz_ref, wgb_ref, bgb_ref,
                  caw_ref, waout_ref, cbw_ref, cbb_ref, lng_ref, lnb_ref, wbout_ref, bbout_ref,
                  wo_ref, gmoe_ref, wr_ref, br_ref,
                  h1_ref, info_ref, cnt_ref, na_ref, nb_ref,
                  cx_scr, glu_scr, conv_scr, n_scr):
    d = D_MODEL
    i = pl.program_id(1)
    streams = range(ns)

    @pl.when(i == 0)
    def _():
        for s in streams:
            cx_scr[s, 0:HALO_A, :] = jnp.zeros((HALO_A, d), F32)
            cx_scr[s, HALO_A - (K_A - 1):HALO_A, :] = sa_ref[s]
            glu_scr[s, 0:HALO_B * NBLK, :] = jnp.zeros((HALO_B * NBLK, LANES), F32)
            _rows_to_time_major(glu_scr.at[s], HALO_B - (K_B - 1), sb_ref[s])

    @pl.when(i > 0)
    def _():
        for s in streams:
            cx_scr[s, 0:HALO_A, :] = cx_scr[s, tl:tl + HALO_A, :]
            glu_scr[s, 0:HALO_B * NBLK, :] = glu_scr[s, tl * NBLK:(tl + HALO_B) * NBLK, :]

    x = [x_ref[s] for s in streams]
    for s in streams:
        n_scr[s] = _rms(x[s], gmix_ref[...]).astype(BF16)

    for s in streams:
        vg = _dot_tiled(n_scr[s], wvg_ref) + bvg_ref[...]
        glu = vg[:, 0:d] * _sigmoid(vg[:, d:2 * d])
        _rows_to_time_major(glu_scr.at[s], HALO_B, glu)

    base_b = HALO_B - (K_B - 1)
    bias = cbb_ref[...]
    group = min(CONV_STEPS, tl)

    def conv_body(c, carry):
        for s in streams:
            t0 = c * group
            accs = [bias] * group
            for j in range(K_B + group - 1):
                xj = glu_scr[s, _ds((t0 + base_b + j) * NBLK, NBLK), :]
                for u in range(group):
                    k = j - u
                    if 0 <= k < K_B:
                        accs[u] = accs[u] + cbw_ref[k * NBLK:(k + 1) * NBLK, :] * xj
            for u in range(group):
                conv_scr[s, _ds((t0 + u) * NBLK, NBLK), :] = accs[u]
        return carry

    lax.fori_loop(0, tl // group, conv_body, 0, unroll=CONV_UNROLL)

    def zseg(s, k):
        return _dot_tiled(n_scr[s], wz_ref.at[k]) + bz_ref[k]

    ya = []
    for s in streams:
        cx = zseg(s, 1) * zseg(s, 2)
        cx_scr[s, HALO_A:HALO_A + tl, :] = cx
        base_a = HALO_A - (K_A - 1)
        conv_a = caw_ref[K_A - 1:K_A, :] * cx
        for k in range(K_A - 1):
            conv_a = conv_a + caw_ref[k:k + 1, :] * cx_scr[s, base_a + k:base_a + k + tl, :]
        ya.append(_dot_tiled((zseg(s, 0) * conv_a).astype(BF16), waout_ref))

    yb = []
    for s in streams:
        cbv = _time_major_to_rows(conv_scr.at[s], 0, tl)
        mu = jnp.mean(cbv, axis=-1, keepdims=True)
        xc = cbv - mu
        var = jnp.mean(xc * xc, axis=-1, keepdims=True)
        ln = xc * lax.rsqrt(var + EPS) * lng_ref[...] + lnb_ref[...]
        cbact = (ln * _sigmoid(ln)).astype(BF16)
        yb.append(_dot_tiled(cbact, wbout_ref) + bbout_ref[...])

    h1 = []
    for s in streams:
        gate_b = _sigmoid(_dot_tiled(n_scr[s], wgb_ref) + bgb_ref[...])
        merged = _sigmoid(zseg(s, 3)) * ya[s] + gate_b * yb[s]
        h1.append(x[s] + _dot_tiled(merged.astype(BF16), wo_ref))
        h1_ref[s] = h1[s]

    @pl.when(i == nt - 1)
    def _():
        for s in streams:
            na_ref[s] = cx_scr[s, tl + HALO_A - (K_A - 1):tl + HALO_A, :]
            nb_ref[s] = _time_major_to_rows(glu_scr.at[s], tl + HALO_B - (K_B - 1), K_B - 1)

    for s in streams:
        xn = _rms(h1[s], gmoe_ref[...]).astype(BF16)
        logits = _dot(xn, wr_ref[...]) + br_ref[...]
        onehot, comb = _route(logits)
        info_ref[s] = onehot + comb
        counts = jnp.sum(onehot, axis=0, keepdims=True)
        cnt_ref[s, 0] = jnp.broadcast_to(counts, (SUBLANES, LANES)).astype(jnp.int32)


def _const_spec(shape):
    zeros = (0,) * len(shape)
    return pl.BlockSpec(shape, lambda b, i: zeros)


def _mixer(x, sa, sb, w, tl):
    nb, l, d = x.shape
    nt = l // tl
    ns = N_STREAMS
    kern = functools.partial(_mixer_kernel, tl, nt, ns)
    weights = [w["g_mix"], w["w_vg"], w["b_vg"], w["w_z"], w["b_z"], w["w_gb"], w["b_gb"],
               w["conv_a_w"], w["w_a_out"], w["conv_b_w"], w["conv_b_b"], w["ln_b_g"], w["ln_b_b"],
               w["w_b_out"], w["b_b_out"], w["w_o"], w["g_moe"], w["w_router"], w["b_router"]]
    in_specs = [
        pl.BlockSpec((ns, tl, d), lambda b, i: (b, i, 0)),
        pl.BlockSpec((ns, K_A - 1, d), lambda b, i: (b, 0, 0)),
        pl.BlockSpec((ns, K_B - 1, d), lambda b, i: (b, 0, 0)),
    ] + [_const_spec(a.shape) for a in weights]
    out_shape = [
        jax.ShapeDtypeStruct((nb, l, d), F32),
        jax.ShapeDtypeStruct((nb, l, LANES), F32),
        jax.ShapeDtypeStruct((nb, nt, SUBLANES, LANES), jnp.int32),
        jax.ShapeDtypeStruct((nb, K_A - 1, d), F32),
        jax.ShapeDtypeStruct((nb, K_B - 1, d), F32),
    ]
    out_specs = [
        pl.BlockSpec((ns, tl, d), lambda b, i: (b, i, 0)),
        pl.BlockSpec((ns, tl, LANES), lambda b, i: (b, i, 0)),
        pl.BlockSpec((ns, 1, SUBLANES, LANES), lambda b, i: (b, i, 0, 0)),
        pl.BlockSpec((ns, K_A - 1, d), lambda b, i: (b, 0, 0)),
        pl.BlockSpec((ns, K_B - 1, d), lambda b, i: (b, 0, 0)),
    ]
    return pl.pallas_call(
        kern,
        grid=(nb // ns, nt),
        in_specs=in_specs,
        out_specs=out_specs,
        out_shape=out_shape,
        scratch_shapes=[
            pltpu.VMEM((ns, HALO_A + tl, d), F32),
            pltpu.VMEM((ns, (HALO_B + tl) * NBLK, LANES), F32),
            pltpu.VMEM((ns, tl * NBLK, LANES), F32),
            pltpu.VMEM((ns, tl, d), BF16),
        ],
        compiler_params=pltpu.CompilerParams(
            dimension_semantics=("arbitrary", "arbitrary"),
            vmem_limit_bytes=VMEM_LIMIT_BYTES),
        name=f"mixer_tl{tl}",
    )(x, sa, sb, *weights)


def _moe_kernel(tb, classes,
                cnt_ref,
                h1_ref, info_ref, p_ref, gmoe_ref, wgu_ref, wd_ref,
                gple_ref, wpg_ref, wpp_ref, gfin_ref,
                out_ref,
                xn_scr, col_scr, row_scr, comb_scr, act_scr, y_scr):
    d = D_MODEL
    i = pl.program_id(0)
    g = pl.program_id(1)
    sizes, chunks = [], []
    for k in range(N_GROUPS):
        cnt = cnt_ref[i * N_GROUPS + k]
        size, num = jnp.int32(classes[0]), (cnt + classes[0] - 1) // classes[0]
        for small, big in zip(classes[:-1], classes[1:]):
            fits = (cnt > 2 * small) & (cnt <= 2 * big)
            size = jnp.where(fits, big, size)
            num = jnp.where(fits, 2, num)
        sizes.append(size)
        chunks.append(num)
    starts = [sum([sizes[j] * chunks[j] for j in range(k)], jnp.int32(0)) for k in range(N_GROUPS + 1)]

    @pl.when((i == 0) & (g == 0))
    def _():
        y_scr[...] = jnp.zeros(y_scr.shape, BF16)

    @pl.when(g == 0)
    def _():
        xn_scr[...] = _rms(h1_ref[...], gmoe_ref[...]).astype(BF16)
        info = info_ref[...]
        lane = lax.broadcasted_iota(jnp.int32, (tb, LANES), 1).astype(F32)
        onehot = jnp.where(lane < N_GROUPS, info, 0.0)
        rb = min(tb, OUT_ROWS)
        tok = lax.broadcasted_iota(jnp.int32, (rb, tb), 1)
        row = lax.broadcasted_iota(jnp.int32, (rb, tb), 0)
        oh16 = onehot.astype(BF16)
        before = jnp.concatenate(
            [_dot(jnp.where(tok < row + r * rb, 1.0, 0.0).astype(BF16), oh16)
             for r in range(tb // rb)], axis=0)
        mypos = jnp.sum(onehot * before, axis=-1, keepdims=True)
        mygrp = jnp.sum(onehot * lane, axis=-1, keepdims=True)
        slot = mypos
        for k in range(1, N_GROUPS):
            slot = slot + jnp.where(mygrp == float(k), starts[k].astype(F32), 0.0)
        col = jnp.where(lane == 0.0, mygrp,
                        jnp.where(lane == 1.0, mypos, jnp.where(lane == 2.0, slot, 0.0)))
        col_scr[...] = col
        row_scr[...] = col.T
        hi = info.astype(BF16)
        comb_scr[:, 0:LANES] = hi
        comb_scr[:, LANES:2 * LANES] = (info - hi.astype(F32)).astype(BF16)

    size, n_chunks, start = sizes[0], chunks[0], starts[0]
    for k in range(1, N_GROUPS):
        size = jnp.where(g == k, sizes[k], size)
        n_chunks = jnp.where(g == k, chunks[k], n_chunks)
        start = jnp.where(g == k, starts[k], start)
    gf = g.astype(F32)
    rsel = jnp.where(row_scr[0:1, :] == gf, row_scr[1:2, :], -1.0)

    def run_chunks(ch):
        sub_iota = lax.broadcasted_iota(jnp.int32, (ch, tb), 0).astype(F32)

        def chunk_body(c, carry):
            base = (c * ch).astype(F32)
            sel = jnp.where(rsel == sub_iota + base, 1.0, 0.0).astype(BF16)
            xs = _dot(sel, xn_scr[...]).astype(BF16)
            cw2 = _dot(sel, comb_scr[...])
            cw = cw2[:, 0:LANES] + cw2[:, LANES:2 * LANES]
            for e in range(EXPERTS_PER_GROUP):
                hgu = _dot_tiled(xs, wgu_ref.at[e])
                gate = hgu[:, 0:D_EXPERT]
                up = hgu[:, D_EXPERT:2 * D_EXPERT]
                a = gate * _sigmoid(gate) * up * cw[:, INFO_COMB_LANE + e:INFO_COMB_LANE + e + 1]
                act_scr[0:ch, e * D_EXPERT:(e + 1) * D_EXPERT] = a.astype(BF16)
            y = _dot_tiled(act_scr[0:ch, :], wd_ref.at[0])
            row0 = pl.multiple_of(start + c * ch, BF16_ROWS)
            y_scr[pl.ds(row0, ch), :] = y.astype(BF16)
            return carry

        lax.fori_loop(0, n_chunks, chunk_body, 0)

    for ch in classes:
        pl.when(size == ch)(functools.partial(run_chunks, ch))

    def epilogue(yrows):
        rb = min(tb, OUT_ROWS)
        lane_iota = lax.broadcasted_iota(jnp.int32, (rb, yrows), 1).astype(F32)
        for r in range(tb // rb):
            rows = slice(r * rb, (r + 1) * rb)
            selt = jnp.where(col_scr[rows, 2:3] == lane_iota, 1.0, 0.0).astype(BF16)
            h2 = h1_ref[rows, :] + _dot(selt, y_scr[0:yrows, :])
            gate = _sigmoid(_dot_tiled(_rms(h2, gple_ref[...]).astype(BF16), wpg_ref))
            h3 = h2 + gate * _dot_tiled(p_ref[rows, :].astype(BF16), wpp_ref)
            out_ref[rows, :] = _rms(h3, gfin_ref[...])

    last = g == N_GROUPS - 1
    yfull = y_scr.shape[0]
    yshort = yfull - MXU_K
    if yshort >= tb:
        pl.when(last & (starts[N_GROUPS] <= yshort))(functools.partial(epilogue, yshort))
        pl.when(last & (starts[N_GROUPS] > yshort))(functools.partial(epilogue, yfull))
    else:
        pl.when(last)(functools.partial(epilogue, yfull))


def _moe(h1, info, p, cnt, w, tb, ch):
    n, d = h1.shape
    ntb = n // tb
    classes = (ch,) + tuple(c for c in PAIR_CHUNKS if c > ch and 2 * c <= tb)
    kern = functools.partial(_moe_kernel, tb, classes)
    yrows = -(-(tb + N_GROUPS * ch) // MXU_K) * MXU_K
    epg = EXPERTS_PER_GROUP
    grid_spec = pltpu.PrefetchScalarGridSpec(
        num_scalar_prefetch=1,
        grid=(ntb, N_GROUPS),
        in_specs=[
            pl.BlockSpec((tb, d), lambda i, g, c: (i, 0)),
            pl.BlockSpec((tb, LANES), lambda i, g, c: (i, 0)),
            pl.BlockSpec((tb, D_PLE), lambda i, g, c: (i, 0)),
            pl.BlockSpec((1, d), lambda i, g, c: (0, 0)),
            pl.BlockSpec((epg, 2 * D_EXPERT // MXU_N, d, MXU_N), lambda i, g, c: (g, 0, 0, 0)),
            pl.BlockSpec((1, d // MXU_N, epg * D_EXPERT, MXU_N), lambda i, g, c: (g, 0, 0, 0)),
            pl.BlockSpec((1, d), lambda i, g, c: (0, 0)),
            pl.BlockSpec((d // MXU_N, d, MXU_N), lambda i, g, c: (0, 0, 0)),
            pl.BlockSpec((d // MXU_N, D_PLE, MXU_N), lambda i, g, c: (0, 0, 0)),
            pl.BlockSpec((1, d), lambda i, g, c: (0, 0)),
        ],
        out_specs=pl.BlockSpec((tb, d), lambda i, g, c: (i, 0)),
        scratch_shapes=[
            pltpu.VMEM((tb, d), BF16),
            pltpu.VMEM((tb, LANES), F32),
            pltpu.VMEM((LANES, tb), F32),
            pltpu.VMEM((tb, 2 * LANES), BF16),
            pltpu.VMEM((max(classes), epg * D_EXPERT), BF16),
            pltpu.VMEM((yrows, d), BF16),
        ],
    )
    return pl.pallas_call(
        kern,
        grid_spec=grid_spec,
        out_shape=jax.ShapeDtypeStruct((n, d), F32),
        compiler_params=pltpu.CompilerParams(
            dimension_semantics=("arbitrary", "arbitrary"),
            vmem_limit_bytes=VMEM_LIMIT_BYTES),
        name=f"moe_tb{tb}",
    )(cnt, h1, info, p, w["g_moe"], w["w_gate_up"], w["w_down"],
      w["g_ple"], w["w_ple_gate"], w["w_ple_proj"], w["g_final"])


def _tiles(nb, l):
    n = nb * l
    tl = min(l, 256)
    tb = min(n, 1024)
    return tl, tb, min(tb, 128)


def _layer(x, p, sa, sb, w):
    nb, l, d = x.shape
    tl, tb, ch = _tiles(nb, l)
    h1, info, cnt, na, nbuf = _mixer(x, sa, sb, w, tl)
    n = nb * l
    cnt = cnt[:, :, 0, :N_GROUPS].reshape(n // tb, tb // tl, N_GROUPS).sum(axis=1).reshape(-1)
    y = _moe(h1.reshape(n, d), info.reshape(n, LANES), p.reshape(n, D_PLE), cnt, w, tb, ch)
    return y.reshape(nb, l, d), na, nbuf


def kernel(x_prompt, x_sample, p_prompt, p_sample, state_conv_a, state_conv_b, g_mix, w_in, b_in,
           conv_a_w, w_a_out, conv_b_w, conv_b_b, ln_b_g, ln_b_b, w_b_out, b_b_out, w_o, g_moe,
           w_group, b_group, w_erouter, b_erouter, w_gate_up, w_down, g_ple, w_ple_gate,
           w_ple_proj, g_final):
    depth = g_mix.shape[0]
    assert depth == 1, "the moe kernel fuses the final norm, so only a one-layer trunk is supported"
    hp, hs = x_prompt, x_sample
    nb = x_prompt.shape[0]
    a_p, b_p, a_s, b_s = [], [], [], []
    row = lambda v: v.reshape(1, -1).astype(F32)
    for li in range(depth):
        pad = LANES - N_GROUPS - N_GROUPS * EXPERTS_PER_GROUP
        w_router = jnp.concatenate(
            [w_group[li], w_erouter[li], jnp.zeros((D_MODEL, pad), F32)], axis=1).astype(BF16)
        b_router = jnp.concatenate([b_group[li], b_erouter[li], jnp.zeros((pad,), F32)]).reshape(1, LANES)
        d = D_MODEL
        win, bi = w_in[li].astype(BF16), b_in[li]
        z_cols = lambda a: jnp.concatenate([a[..., 0:3 * d], a[..., 5 * d:6 * d]], axis=-1)
        w = dict(
            g_mix=row(g_mix[li]),
            w_vg=_col_tiles(win[:, 3 * d:5 * d]), b_vg=row(bi[3 * d:5 * d]),
            w_z=jnp.stack([_col_tiles(z_cols(win)[:, k * d:(k + 1) * d]) for k in range(Z_SEGS)]),
            b_z=z_cols(bi).reshape(Z_SEGS, 1, d),
            w_gb=_col_tiles(win[:, 6 * d:7 * d]), b_gb=row(bi[6 * d:7 * d]),
            conv_a_w=conv_a_w[li], w_a_out=_col_tiles(w_a_out[li].astype(BF16)),
            conv_b_w=conv_b_w[li].reshape(K_B * NBLK, LANES),
            conv_b_b=conv_b_b[li].reshape(NBLK, LANES), ln_b_g=row(ln_b_g[li]), ln_b_b=row(ln_b_b[li]),
            w_b_out=_col_tiles(w_b_out[li].astype(BF16)), b_b_out=row(b_b_out[li]),
            w_o=_col_tiles(w_o[li].astype(BF16)),
            g_moe=row(g_moe[li]), w_router=w_router, b_router=b_router,
            w_gate_up=jax.vmap(_col_tiles)(w_gate_up[li].astype(BF16)),
            w_down=jax.vmap(_col_tiles)(w_down[li].astype(BF16).reshape(
                N_GROUPS, EXPERTS_PER_GROUP * D_EXPERT, d)),
            g_ple=row(g_ple[li]), w_ple_gate=_col_tiles(w_ple_gate[li].astype(BF16)),
            w_ple_proj=_col_tiles(w_ple_proj[li].astype(BF16)), g_final=row(g_final))
        zero_a = jnp.zeros((nb, K_A - 1, D_MODEL), F32)
        zero_b = jnp.zeros((nb, K_B - 1, D_MODEL), F32)
        hp, na_p, nb_p = _layer(hp, p_prompt[li], zero_a, zero_b, w)
        hs, na_s, nb_s = _layer(hs, p_sample[li], state_conv_a[li], state_conv_b[li], w)
        a_p.append(na_p)
        b_p.append(nb_p)
        a_s.append(na_s)
        b_s.append(nb_s)
    return (hp, hs, jnp.stack(a_p), jnp.stack(b_p), jnp.stack(a_s), jnp.stack(b_s))
```

```python
import functools

import jax
import jax.numpy as jnp
from jax import lax
from jax.experimental import pallas as pl
from jax.experimental.pallas import tpu as pltpu

D_MODEL = 1024
D_PLE = 256
K_A = 3
K_B = 31
N_GROUPS = 4
EXPERTS_PER_GROUP = 8
D_EXPERT = 256
EPS = 1e-6

MXU_K = 256
MXU_N = 256
LANES = 128
SUBLANES = 8
HALO_A = 8
HALO_B = 32
NBLK = D_MODEL // LANES
CONV_STEPS = 8
CONV_UNROLL = 2
Z_SEGS = 4
N_STREAMS = 2
PAIR_CHUNKS = (144, 160)
BF16_ROWS = 16
OUT_ROWS = 256
INFO_GROUP_LANE = 0
INFO_COMB_LANE = 8
VMEM_LIMIT_BYTES = 60 * 1024 * 1024

F32 = jnp.float32
BF16 = jnp.bfloat16
NEG_BIG = -1e30
NEG_LOG2_E = -1.4426950408889634


def _rms(x, g):
    ms = jnp.mean(x * x, axis=-1, keepdims=True)
    return x * lax.rsqrt(ms + EPS) * g


def _sigmoid(x):
    return 1.0 / (1.0 + jnp.exp2(x * NEG_LOG2_E))


def _dot(a, b):
    return jnp.dot(a, b, preferred_element_type=F32)


def _col_tiles(w):
    k, n = w.shape
    return w.reshape(k, n // MXU_N, MXU_N).transpose(1, 0, 2)


def _dot_tiled(a, w_ref):
    return jnp.concatenate([_dot(a, w_ref[q]) for q in range(w_ref.shape[0])], axis=-1)


def _rows_to_time_major(dst_ref, t0, val):
    r = val.shape[0]
    for j in range(NBLK):
        dst_ref[pl.ds(t0 * NBLK + j, r, stride=NBLK), :] = val[:, j * LANES:(j + 1) * LANES]


def _ds(start, size):
    if isinstance(start, int):
        return pl.ds(start, size)
    return pl.ds(pl.multiple_of(start, size), size)


def _time_major_to_rows(src_ref, t0, r):
    return jnp.concatenate(
        [src_ref[pl.ds(t0 * NBLK + j, r, stride=NBLK), :] for j in range(NBLK)], axis=-1)


def _route(logits):
    r = logits.shape[0]
    lane = lax.broadcasted_iota(jnp.int32, (r, LANES), 1).astype(F32)
    is_g = lane < N_GROUPS
    gl = jnp.where(is_g, logits, NEG_BIG)
    gmax = jnp.max(gl, axis=-1, keepdims=True)
    gidx = jnp.min(jnp.where(gl == gmax, lane, float(LANES)), axis=-1, keepdims=True)
    gsum = jnp.sum(jnp.where(is_g, jnp.exp(gl - gmax), 0.0), axis=-1, keepdims=True)
    p_g = 1.0 / gsum
    lo = float(N_GROUPS) + gidx * float(EXPERTS_PER_GROUP)
    in_grp = (lane >= lo) & (lane < lo + float(EXPERTS_PER_GROUP))
    el = jnp.where(in_grp, logits, NEG_BIG)
    m1 = jnp.max(el, axis=-1, keepdims=True)
    i1 = jnp.min(jnp.where(el == m1, lane, float(LANES)), axis=-1, keepdims=True)
    el2 = jnp.where(lane == i1, NEG_BIG, el)
    m2 = jnp.max(el2, axis=-1, keepdims=True)
    i2 = jnp.min(jnp.where(el2 == m2, lane, float(LANES)), axis=-1, keepdims=True)
    e2 = jnp.exp(m2 - m1)
    w1 = 1.0 / (1.0 + e2)
    w2 = e2 * w1
    shift = float(INFO_COMB_LANE) - lo
    onehot = jnp.where(lane == gidx + float(INFO_GROUP_LANE), 1.0, 0.0)
    comb = (jnp.where(lane == i1 + shift, p_g * w1, 0.0)
            + jnp.where(lane == i2 + shift, p_g * w2, 0.0))
    return onehot, comb


def _mixer_kernel(tl, nt, ns,
                  x_ref, sa_ref, sb_ref, gmix_ref, wvg_ref, bvg_ref, wz_ref, bz_ref, wgb_ref, bgb_ref,
                  caw_ref, waout_ref, cbw_ref, cbb_ref, lng_ref, lnb_ref, wbout_ref, bbout_ref,
                  wo_ref, gmoe_ref, wr_ref, br_ref,
                  h1_ref, info_ref, cnt_ref, na_ref, nb_ref,
                  cx_scr, glu_scr, conv_scr, n_scr):
    d = D_MODEL
    i = pl.program_id(1)
    streams = range(ns)

    @pl.when(i == 0)
    def _():
        for s in streams:
            cx_scr[s, 0:HALO_A, :] = jnp.zeros((HALO_A, d), F32)
            cx_scr[s, HALO_A - (K_A - 1):HALO_A, :] = sa_ref[s]
            glu_scr[s, 0:HALO_B * NBLK, :] = jnp.zeros((HALO_B * NBLK, LANES), F32)
            _rows_to_time_major(glu_scr.at[s], HALO_B - (K_B - 1), sb_ref[s])

    @pl.when(i > 0)
    def _():
        for s in streams:
            cx_scr[s, 0:HALO_A, :] = cx_scr[s, tl:tl + HALO_A, :]
            glu_scr[s, 0:HALO_B * NBLK, :] = glu_scr[s, tl * NBLK:(tl + HALO_B) * NBLK, :]

    x = [x_ref[s] for s in streams]
    for s in streams:
        n_scr[s] = _rms(x[s], gmix_ref[...]).astype(BF16)

    for s in streams:
        vg = _dot_tiled(n_scr[s], wvg_ref) + bvg_ref[...]
        glu = vg[:, 0:d] * _sigmoid(vg[:, d:2 * d])
        _rows_to_time_major(glu_scr.at[s], HALO_B, glu)

    base_b = HALO_B - (K_B - 1)
    bias = cbb_ref[...]
    group = min(CONV_STEPS, tl)

    def conv_body(c, carry):
        for s in streams:
            t0 = c * group
            accs = [bias] * group
            for j in range(K_B + group - 1):
                xj = glu_scr[s, _ds((t0 + base_b + j) * NBLK, NBLK), :]
                for u in range(group):
                    k = j - u
                    if 0 <= k < K_B:
                        accs[u] = accs[u] + cbw_ref[k * NBLK:(k + 1) * NBLK, :] * xj
            for u in range(group):
                conv_scr[s, _ds((t0 + u) * NBLK, NBLK), :] = accs[u]
        return carry

    lax.fori_loop(0, tl // group, conv_body, 0, unroll=CONV_UNROLL)

    def zseg(s, k):
        return _dot_tiled(n_scr[s], wz_ref.at[k]) + bz_ref[k]

    ya = []
    for s in streams:
        cx = zseg(s, 1) * zseg(s, 2)
        cx_scr[s, HALO_A:HALO_A + tl, :] = cx
        base_a = HALO_A - (K_A - 1)
        conv_a = caw_ref[K_A - 1:K_A, :] * cx
        for k in range(K_A - 1):
            conv_a = conv_a + caw_ref[k:k + 1, :] * cx_scr[s, base_a + k:base_a + k + tl, :]
        ya.append(_dot_tiled((zseg(s, 0) * conv_a).astype(BF16), waout_ref))

    yb = []
    for s in streams:
        cbv = _time_major_to_rows(conv_scr.at[s], 0, tl)
        mu = jnp.mean(cbv, axis=-1, keepdims=True)
        xc = cbv - mu
        var = jnp.mean(xc * xc, axis=-1, keepdims=True)
        ln = xc * lax.rsqrt(var + EPS) * lng_ref[...] + lnb_ref[...]
        cbact = (ln * _sigmoid(ln)).astype(BF16)
        yb.append(_dot_tiled(cbact, wbout_ref) + bbout_ref[...])

    h1 = []
    for s in streams:
        gate_b = _sigmoid(_dot_tiled(n_scr[s], wgb_ref) + bgb_ref[...])
        merged = _sigmoid(zseg(s, 3)) * ya[s] + gate_b * yb[s]
        h1.append(x[s] + _dot_tiled(merged.astype(BF16), wo_ref))
        h1_ref[s] = h1[s]

    @pl.when(i == nt - 1)
    def _():
        for s in streams:
            na_ref[s] = cx_scr[s, tl + HALO_A - (K_A - 1):tl + HALO_A, :]
            nb_ref[s] = _time_major_to_rows(glu_scr.at[s], tl + HALO_B - (K_B - 1), K_B - 1)

    for s in streams:
        xn = _rms(h1[s], gmoe_ref[...]).astype(BF16)
        logits = _dot(xn, wr_ref[...]) + br_ref[...]
        onehot, comb = _route(logits)
        info_ref[s] = onehot + comb
        counts = jnp.sum(onehot, axis=0, keepdims=True)
        cnt_ref[s, 0] = jnp.broadcast_to(counts, (SUBLANES, LANES)).astype(jnp.int32)


def _const_spec(shape):
    zeros = (0,) * len(shape)
    return pl.BlockSpec(shape, lambda b, i: zeros)


def _mixer(x, sa, sb, w, tl):
    nb, l, d = x.shape
    nt = l // tl
    ns = N_STREAMS
    kern = functools.partial(_mixer_kernel, tl, nt, ns)
    weights = [w["g_mix"], w["w_vg"], w["b_vg"], w["w_z"], w["b_z"], w["w_gb"], w["b_gb"],
               w["conv_a_w"], w["w_a_out"], w["conv_b_w"], w["conv_b_b"], w["ln_b_g"], w["ln_b_b"],
               w["w_b_out"], w["b_b_out"], w["w_o"], w["g_moe"], w["w_router"], w["b_router"]]
    in_specs = [
        pl.BlockSpec((ns, tl, d), lambda b, i: (b, i, 0)),
        pl.BlockSpec((ns, K_A - 1, d), lambda b, i: (b, 0, 0)),
        pl.BlockSpec((ns, K_B - 1, d), lambda b, i: (b, 0, 0)),
    ] + [_const_spec(a.shape) for a in weights]
    out_shape = [
        jax.ShapeDtypeStruct((nb, l, d), F32),
        jax.ShapeDtypeStruct((nb, l, LANES), F32),
        jax.ShapeDtypeStruct((nb, nt, SUBLANES, LANES), jnp.int32),
        jax.ShapeDtypeStruct((nb, K_A - 1, d), F32),
        jax.ShapeDtypeStruct((nb, K_B - 1, d), F32),
    ]
    out_specs = [
        pl.BlockSpec((ns, tl, d), lambda b, i: (b, i, 0)),
        pl.BlockSpec((ns, tl, LANES), lambda b, i: (b, i, 0)),
        pl.BlockSpec((ns, 1, SUBLANES, LANES), lambda b, i: (b, i, 0, 0)),
        pl.BlockSpec((ns, K_A - 1, d), lambda b, i: (b, 0, 0)),
        pl.BlockSpec((ns, K_B - 1, d), lambda b, i: (b, 0, 0)),
    ]
    return pl.pallas_call(
        kern,
        grid=(nb // ns, nt),
        in_specs=in_specs,
        out_specs=out_specs,
        out_shape=out_shape,
        scratch_shapes=[
            pltpu.VMEM((ns, HALO_A + tl, d), F32),
            pltpu.VMEM((ns, (HALO_B + tl) * NBLK, LANES), F32),
            pltpu.VMEM((ns, tl * NBLK, LANES), F32),
            pltpu.VMEM((ns, tl, d), BF16),
        ],
        compiler_params=pltpu.CompilerParams(
            dimension_semantics=("arbitrary", "arbitrary"),
            vmem_limit_bytes=VMEM_LIMIT_BYTES),
        name=f"mixer_tl{tl}",
    )(x, sa, sb, *weights)


def _moe_kernel(tb, classes,
                cnt_ref,
                h1_ref, info_ref, p_ref, gmoe_ref, wgu_ref, wd_ref,
                gple_ref, wpg_ref, wpp_ref, gfin_ref,
                out_ref,
                xn_scr, col_scr, row_scr, comb_scr, act_scr, y_scr):
    d = D_MODEL
    i = pl.program_id(0)
    g = pl.program_id(1)
    sizes, chunks = [], []
    for k in range(N_GROUPS):
        cnt = cnt_ref[i * N_GROUPS + k]
        size, num = jnp.int32(classes[0]), (cnt + classes[0] - 1) // classes[0]
        for small, big in zip(classes[:-1], classes[1:]):
            fits = (cnt > 2 * small) & (cnt <= 2 * big)
            size = jnp.where(fits, big, size)
            num = jnp.where(fits, 2, num)
        sizes.append(size)
        chunks.append(num)
    starts = [sum([sizes[j] * chunks[j] for j in range(k)], jnp.int32(0)) for k in range(N_GROUPS + 1)]

    @pl.when((i == 0) & (g == 0))
    def _():
        y_scr[...] = jnp.zeros(y_scr.shape, BF16)

    @pl.when(g == 0)
    def _():
        xn_scr[...] = _rms(h1_ref[...], gmoe_ref[...]).astype(BF16)
        info = info_ref[...]
        lane = lax.broadcasted_iota(jnp.int32, (tb, LANES), 1).astype(F32)
        onehot = jnp.where(lane < N_GROUPS, info, 0.0)
        rb = min(tb, OUT_ROWS)
        tok = lax.broadcasted_iota(jnp.int32, (rb, tb), 1)
        row = lax.broadcasted_iota(jnp.int32, (rb, tb), 0)
        oh16 = onehot.astype(BF16)
        before = jnp.concatenate(
            [_dot(jnp.where(tok < row + r * rb, 1.0, 0.0).astype(BF16), oh16)
             for r in range(tb // rb)], axis=0)
        mypos = jnp.sum(onehot * before, axis=-1, keepdims=True)
        mygrp = jnp.sum(onehot * lane, axis=-1, keepdims=True)
        slot = mypos
        for k in range(1, N_GROUPS):
            slot = slot + jnp.where(mygrp == float(k), starts[k].astype(F32), 0.0)
        col = jnp.where(lane == 0.0, mygrp,
                        jnp.where(lane == 1.0, mypos, jnp.where(lane == 2.0, slot, 0.0)))
        col_scr[...] = col
        row_scr[...] = col.T
        hi = info.astype(BF16)
        comb_scr[:, 0:LANES] = hi
        comb_scr[:, LANES:2 * LANES] = (info - hi.astype(F32)).astype(BF16)

    size, n_chunks, start = sizes[0], chunks[0], starts[0]
    for k in range(1, N_GROUPS):
        size = jnp.where(g == k, sizes[k], size)
        n_chunks = jnp.where(g == k, chunks[k], n_chunks)
        start = jnp.where(g == k, starts[k], start)
    gf = g.astype(F32)
    rsel = jnp.where(row_scr[0:1, :] == gf, row_scr[1:2, :], -1.0)

    def run_chunks(ch):
        sub_iota = lax.broadcasted_iota(jnp.int32, (ch, tb), 0).astype(F32)

        def chunk_body(c, carry):
            base = (c * ch).astype(F32)
            sel = jnp.where(rsel == sub_iota + base, 1.0, 0.0).astype(BF16)
            xs = _dot(sel, xn_scr[...]).astype(BF16)
            cw2 = _dot(sel, comb_scr[...])
            cw = cw2[:, 0:LANES] + cw2[:, LANES:2 * LANES]
            for e in range(EXPERTS_PER_GROUP):
                hgu = _dot(xs, wgu_ref[e])
                gate = hgu[:, 0:D_EXPERT]
                up = hgu[:, D_EXPERT:2 * D_EXPERT]
                a = gate * _sigmoid(gate) * up * cw[:, INFO_COMB_LANE + e:INFO_COMB_LANE + e + 1]
                act_scr[0:ch, e * D_EXPERT:(e + 1) * D_EXPERT] = a.astype(BF16)
            y = _dot(act_scr[0:ch, :], wd_ref[0])
            row0 = pl.multiple_of(start + c * ch, BF16_ROWS)
            y_scr[pl.ds(row0, ch), :] = y.astype(BF16)
            return carry

        lax.fori_loop(0, n_chunks, chunk_body, 0)

    for ch in classes:
        pl.when(size == ch)(functools.partial(run_chunks, ch))

    def epilogue(yrows):
        rb = min(tb, OUT_ROWS)
        lane_iota = lax.broadcasted_iota(jnp.int32, (rb, yrows), 1).astype(F32)
        for r in range(tb // rb):
            rows = slice(r * rb, (r + 1) * rb)
            selt = jnp.where(col_scr[rows, 2:3] == lane_iota, 1.0, 0.0).astype(BF16)
            h2 = h1_ref[rows, :] + _dot(selt, y_scr[0:yrows, :])
            gate = _sigmoid(_dot_tiled(_rms(h2, gple_ref[...]).astype(BF16), wpg_ref))
            h3 = h2 + gate * _dot_tiled(p_ref[rows, :].astype(BF16), wpp_ref)
            out_ref[rows, :] = _rms(h3, gfin_ref[...])

    last = g == N_GROUPS - 1
    yfull = y_scr.shape[0]
    yshort = yfull - MXU_K
    if yshort >= tb:
        pl.when(last & (starts[N_GROUPS] <= yshort))(functools.partial(epilogue, yshort))
        pl.when(last & (starts[N_GROUPS] > yshort))(functools.partial(epilogue, yfull))
    else:
        pl.when(last)(functools.partial(epilogue, yfull))


def _moe(h1, info, p, cnt, w, tb, ch):
    n, d = h1.shape
    ntb = n // tb
    classes = (ch,) + tuple(c for c in PAIR_CHUNKS if c > ch and 2 * c <= tb)
    kern = functools.partial(_moe_kernel, tb, classes)
    yrows = -(-(tb + N_GROUPS * ch) // MXU_K) * MXU_K
    epg = EXPERTS_PER_GROUP
    grid_spec = pltpu.PrefetchScalarGridSpec(
        num_scalar_prefetch=1,
        grid=(ntb, N_GROUPS),
        in_specs=[
            pl.BlockSpec((tb, d), lambda i, g, c: (i, 0)),
            pl.BlockSpec((tb, LANES), lambda i, g, c: (i, 0)),
            pl.BlockSpec((tb, D_PLE), lambda i, g, c: (i, 0)),
            pl.BlockSpec((1, d), lambda i, g, c: (0, 0)),
            pl.BlockSpec((epg, d, 2 * D_EXPERT), lambda i, g, c: (g, 0, 0)),
            pl.BlockSpec((1, epg * D_EXPERT, d), lambda i, g, c: (g, 0, 0)),
            pl.BlockSpec((1, d), lambda i, g, c: (0, 0)),
            pl.BlockSpec((d // MXU_N, d, MXU_N), lambda i, g, c: (0, 0, 0)),
            pl.BlockSpec((d // MXU_N, D_PLE, MXU_N), lambda i, g, c: (0, 0, 0)),
            pl.BlockSpec((1, d), lambda i, g, c: (0, 0)),
        ],
        out_specs=pl.BlockSpec((tb, d), lambda i, g, c: (i, 0)),
        scratch_shapes=[
            pltpu.VMEM((tb, d), BF16),
            pltpu.VMEM((tb, LANES), F32),
            pltpu.VMEM((LANES, tb), F32),
            pltpu.VMEM((tb, 2 * LANES), BF16),
            pltpu.VMEM((max(classes), epg * D_EXPERT), BF16),
            pltpu.VMEM((yrows, d), BF16),
        ],
    )
    return pl.pallas_call(
        kern,
        grid_spec=grid_spec,
        out_shape=jax.ShapeDtypeStruct((n, d), F32),
        compiler_params=pltpu.CompilerParams(
            dimension_semantics=("arbitrary", "arbitrary"),
            vmem_limit_bytes=VMEM_LIMIT_BYTES),
        name=f"moe_tb{tb}",
    )(cnt, h1, info, p, w["g_moe"], w["w_gate_up"], w["w_down"],
      w["g_ple"], w["w_ple_gate"], w["w_ple_proj"], w["g_final"])


def _tiles(nb, l):
    n = nb * l
    tl = min(l, 256)
    tb = min(n, 1024)
    return tl, tb, min(tb, 128)


def _layer(x, p, sa, sb, w):
    nb, l, d = x.shape
    tl, tb, ch = _tiles(nb, l)
    h1, info, cnt, na, nbuf = _mixer(x, sa, sb, w, tl)
    n = nb * l
    cnt = cnt[:, :, 0, :N_GROUPS].reshape(n // tb, tb // tl, N_GROUPS).sum(axis=1).reshape(-1)
    y = _moe(h1.reshape(n, d), info.reshape(n, LANES), p.reshape(n, D_PLE), cnt, w, tb, ch)
    return y.reshape(nb, l, d), na, nbuf


def kernel(x_prompt, x_sample, p_prompt, p_sample, state_conv_a, state_conv_b, g_mix, w_in, b_in,
           conv_a_w, w_a_out, conv_b_w, conv_b_b, ln_b_g, ln_b_b, w_b_out, b_b_out, w_o, g_moe,
           w_group, b_group, w_erouter, b_erouter, w_gate_up, w_down, g_ple, w_ple_gate,
           w_ple_proj, g_final):
    depth = g_mix.shape[0]
    assert depth == 1, "the moe kernel fuses the final norm, so only a one-layer trunk is supported"
    hp, hs = x_prompt, x_sample
    nb = x_prompt.shape[0]
    a_p, b_p, a_s, b_s = [], [], [], []
    row = lambda v: v.reshape(1, -1).astype(F32)
    for li in range(depth):
        pad = LANES - N_GROUPS - N_GROUPS * EXPERTS_PER_GROUP
        w_router = jnp.concatenate(
            [w_group[li], w_erouter[li], jnp.zeros((D_MODEL, pad), F32)], axis=1).astype(BF16)
        b_router = jnp.concatenate([b_group[li], b_erouter[li], jnp.zeros((pad,), F32)]).reshape(1, LANES)
        d = D_MODEL
        win, bi = w_in[li].astype(BF16), b_in[li]
        z_cols = lambda a: jnp.concatenate([a[..., 0:3 * d], a[..., 5 * d:6 * d]], axis=-1)
        w = dict(
            g_mix=row(g_mix[li]),
            w_vg=_col_tiles(win[:, 3 * d:5 * d]), b_vg=row(bi[3 * d:5 * d]),
            w_z=jnp.stack([_col_tiles(z_cols(win)[:, k * d:(k + 1) * d]) for k in range(Z_SEGS)]),
            b_z=z_cols(bi).reshape(Z_SEGS, 1, d),
            w_gb=_col_tiles(win[:, 6 * d:7 * d]), b_gb=row(bi[6 * d:7 * d]),
            conv_a_w=conv_a_w[li], w_a_out=_col_tiles(w_a_out[li].astype(BF16)),
            conv_b_w=conv_b_w[li].reshape(K_B * NBLK, LANES),
            conv_b_b=conv_b_b[li].reshape(NBLK, LANES), ln_b_g=row(ln_b_g[li]), ln_b_b=row(ln_b_b[li]),
            w_b_out=_col_tiles(w_b_out[li].astype(BF16)), b_b_out=row(b_b_out[li]),
            w_o=_col_tiles(w_o[li].astype(BF16)),
            g_moe=row(g_moe[li]), w_router=w_router, b_router=b_router,
            w_gate_up=w_gate_up[li].astype(BF16),
            w_down=w_down[li].astype(BF16).reshape(N_GROUPS, EXPERTS_PER_GROUP * D_EXPERT, d),
            g_ple=row(g_ple[li]), w_ple_gate=_col_tiles(w_ple_gate[li].astype(BF16)),
            w_ple_proj=_col_tiles(w_ple_proj[li].astype(BF16)), g_final=row(g_final))
        zero_a = jnp.zeros((nb, K_A - 1, D_MODEL), F32)
        zero_b = jnp.zeros((nb, K_B - 1, D_MODEL), F32)
        hp, na_p, nb_p = _layer(hp, p_prompt[li], zero_a, zero_b, w)
        hs, na_s, nb_s = _layer(hs, p_sample[li], state_conv_a[li], state_conv_b[li], w)
        a_p.append(na_p)
        b_p.append(nb_p)
        a_s.append(na_s)
        b_s.append(nb_s)
    return (hp, hs, jnp.stack(a_p), jnp.stack(b_p), jnp.stack(a_s), jnp.stack(b_s))
```
